```python
import jax, jax.numpy as jnp
from jax import lax
import numpy as np

D_MODEL = 2048
BATCH = 4
SEQ = 4096
DEPTH = 2

N_HEADS_MLA = 16
QK_NOPE_DIM = 128
QK_ROPE_DIM = 64
V_HEAD_DIM = 128
Q_LORA_RANK = 512
KV_LORA_RANK = 512
ROPE_THETA = 10000.0
Q_BLOCK = 128

LRU_WIDTH = D_MODEL
LRU_HEADS = 16
LRU_BLOCK = LRU_WIDTH // LRU_HEADS
CONV_WIDTH = 4
LRU_C = 8.0

D_FF = 4 * D_MODEL
PLE_DIM = 256
EPS = 1e-6

IN_SPLITS = (Q_LORA_RANK, KV_LORA_RANK + QK_ROPE_DIM, LRU_WIDTH, LRU_WIDTH, D_MODEL, D_MODEL)
D_IN = Q_LORA_RANK + KV_LORA_RANK + QK_ROPE_DIM + 2 * LRU_WIDTH + 2 * D_MODEL

kernel_name = 'hybrid_mla_rglru_gated_parallel'


def rms_norm(x, g):
    xf = x.astype(jnp.float32)
    y = xf * lax.rsqrt(jnp.mean(xf * xf, axis=-1, keepdims=True) + EPS)
    return (y * g.astype(jnp.float32)).astype(x.dtype)


def rope_tables(positions):
    half = QK_ROPE_DIM // 2
    inv_freq = jnp.power(jnp.float32(ROPE_THETA), -jnp.arange(half, dtype=jnp.float32) * (2.0 / QK_ROPE_DIM))
    ang = positions.astype(jnp.float32)[..., None] * inv_freq
    return jnp.cos(ang), jnp.sin(ang)


def apply_rope(t, cos, sin):
    half = QK_ROPE_DIM // 2
    t1 = t[..., :half].astype(jnp.float32)
    t2 = t[..., half:].astype(jnp.float32)
    out = jnp.concatenate([t1 * cos - t2 * sin, t2 * cos + t1 * sin], axis=-1)
    return out.astype(t.dtype)


def split_columns(z):
    parts = []
    start = 0
    for width in IN_SPLITS:
        parts.append(z[..., start:start + width])
        start += width
    return parts


def mla_branch(cq_raw, ckv_raw, g_q, w_qb, g_kv, w_kvb, cos, sin):
    b, s, _ = cq_raw.shape
    cq = rms_norm(cq_raw, g_q)
    q = (cq @ w_qb).reshape(b, s, N_HEADS_MLA, QK_NOPE_DIM + QK_ROPE_DIM)
    q_nope = q[..., :QK_NOPE_DIM]
    q_pe = apply_rope(q[..., QK_NOPE_DIM:], cos[:, :, None, :], sin[:, :, None, :])
    c_kv = rms_norm(ckv_raw[..., :KV_LORA_RANK], g_kv)
    k_pe = apply_rope(ckv_raw[..., KV_LORA_RANK:], cos, sin)
    kv = (c_kv @ w_kvb).reshape(b, s, N_HEADS_MLA, QK_NOPE_DIM + V_HEAD_DIM)
    k_nope = kv[..., :QK_NOPE_DIM]
    v = kv[..., QK_NOPE_DIM:]
    scale = (QK_NOPE_DIM + QK_ROPE_DIM) ** -0.5
    outs = []
    for blk in range(s // Q_BLOCK):
        s0 = blk * Q_BLOCK
        s1 = s0 + Q_BLOCK
        sc = (jnp.einsum('bqhd,bkhd->bhqk', q_nope[:, s0:s1], k_nope[:, :s1])
              + jnp.einsum('bqhr,bkr->bhqk', q_pe[:, s0:s1], k_pe[:, :s1])).astype(jnp.float32) * scale
        causal = jnp.arange(s1)[None, :] <= jnp.arange(s0, s1)[:, None]
        sc = jnp.where(causal, sc, -jnp.inf)
        pr = jax.nn.softmax(sc, axis=-1).astype(v.dtype)
        outs.append(jnp.einsum('bhqk,bkhd->bqhd', pr, v[:, :s1]))
    o = jnp.concatenate(outs, axis=1)
    return o.reshape(b, s, N_HEADS_MLA * V_HEAD_DIM)


def rglru_branch(xb, yb, conv_w, conv_b, w_a, b_a, w_x, b_x, lru_lambda):
    b, s, _ = xb.shape
    xc = lax.conv_general_dilated(
        xb, conv_w[:, None, :], window_strides=(1,), padding=[(CONV_WIDTH - 1, 0)],
        dimension_numbers=('NWC', 'WIO', 'NWC'), feature_group_count=LRU_WIDTH) + conv_b
    xh = xc.reshape(b, s, LRU_HEADS, LRU_BLOCK)
    r = jax.nn.sigmoid(jnp.einsum('bshi,hij->bshj', xh, w_a).reshape(b, s, LRU_WIDTH) + b_a)
    i = jax.nn.sigmoid(jnp.einsum('bshi,hij->bshj', xh, w_x).reshape(b, s, LRU_WIDTH) + b_x)
    log_a = -LRU_C * r.astype(jnp.float32) * jax.nn.softplus(-lru_lambda.astype(jnp.float32))
    a = jnp.exp(log_a)
    gated_x = jnp.sqrt(-jnp.expm1(2.0 * log_a)) * (i * xc).astype(jnp.float32)

    def combine(left, right):
        a1, h1 = left
        a2, h2 = right
        return a1 * a2, a2 * h1 + h2

    _, h = lax.associative_scan(combine, (a, gated_x), axis=1)
    return jax.nn.gelu(yb, approximate=True) * h.astype(xb.dtype)


def setup_inputs(seed: int = 0) -> dict:
    key = jax.random.key(seed)
    ks = jax.random.split(key, 26)
    f32 = jnp.float32

    def nrm(k, shape, fan_in):
        return jax.random.normal(k, shape, f32) * (fan_in ** -0.5)

    def gain(k, shape):
        return 1.0 + 0.02 * jax.random.normal(k, shape, f32)

    u = jax.random.uniform(ks[13], (DEPTH, LRU_WIDTH), f32, minval=0.9, maxval=0.999)
    a0 = u ** (1.0 / LRU_C)
    lru_lambda = jnp.log(a0) - jnp.log1p(-a0)
    return {
        'x': jax.random.normal(ks[0], (BATCH, SEQ, D_MODEL), f32),
        'p': jax.random.normal(ks[1], (DEPTH, BATCH, SEQ, PLE_DIM), f32),
        'positions': jnp.broadcast_to(jnp.arange(SEQ, dtype=jnp.int32)[None, :], (BATCH, SEQ)),
        'g_mix': gain(ks[2], (DEPTH, D_MODEL)),
        'w_in': nrm(ks[3], (DEPTH, D_MODEL, D_IN), D_MODEL),
        'g_q': gain(ks[4], (DEPTH, Q_LORA_RANK)),
        'w_qb': nrm(ks[5], (DEPTH, Q_LORA_RANK, N_HEADS_MLA * (QK_NOPE_DIM + QK_ROPE_DIM)), Q_LORA_RANK),
        'g_kv': gain(ks[6], (DEPTH, KV_LORA_RANK)),
        'w_kvb': nrm(ks[7], (DEPTH, KV_LORA_RANK, N_HEADS_MLA * (QK_NOPE_DIM + V_HEAD_DIM)), KV_LORA_RANK),
        'conv_w': nrm(ks[8], (DEPTH, CONV_WIDTH, LRU_WIDTH), CONV_WIDTH),
        'conv_b': 0.01 * jax.random.normal(ks[9], (DEPTH, LRU_WIDTH), f32),
        'w_a': nrm(ks[10], (DEPTH, LRU_HEADS, LRU_BLOCK, LRU_BLOCK), LRU_BLOCK),
        'b_a': 0.01 * jax.random.normal(ks[11], (DEPTH, LRU_WIDTH), f32),
        'w_x': nrm(ks[12], (DEPTH, LRU_HEADS, LRU_BLOCK, LRU_BLOCK), LRU_BLOCK),
        'b_x': 0.01 * jax.random.normal(ks[14], (DEPTH, LRU_WIDTH), f32),
        'lru_lambda': lru_lambda,
        'w_out': nrm(ks[15], (DEPTH, D_MODEL, D_MODEL), D_MODEL),
        'g_mlp': gain(ks[16], (DEPTH, D_MODEL)),
        'w_up': nrm(ks[17], (DEPTH, D_MODEL, D_FF), D_MODEL),
        'w_down': nrm(ks[18], (DEPTH, D_FF, D_MODEL), D_FF),
        'g_ple': gain(ks[19], (DEPTH, D_MODEL)),
        'w_ple_gate': nrm(ks[20], (DEPTH, D_MODEL, D_MODEL), D_MODEL),
        'w_ple_proj': nrm(ks[21], (DEPTH, PLE_DIM, D_MODEL), PLE_DIM),
        'g_final': gain(ks[22], (D_MODEL,)),
    }


def reference(x, p, positions, g_mix, w_in, g_q, w_qb, g_kv, w_kvb, conv_w, conv_b, w_a, b_a,
              w_x, b_x, lru_lambda, w_out, g_mlp, w_up, w_down, g_ple, w_ple_gate, w_ple_proj, g_final):
    cos, sin = rope_tables(positions)
    for l in range(DEPTH):
        u = rms_norm(x, g_mix[l])
        cq_raw, ckv_raw, xb, yb, gate_a, gate_r = split_columns(u @ w_in[l])
        attn = mla_branch(cq_raw, ckv_raw, g_q[l], w_qb[l], g_kv[l], w_kvb[l], cos, sin)
        rec = rglru_branch(xb, yb, conv_w[l], conv_b[l], w_a[l], b_a[l], w_x[l], b_x[l], lru_lambda[l])
        merged = jax.nn.sigmoid(gate_a) * attn + jax.nn.sigmoid(gate_r) * rec
        x = x + merged @ w_out[l]
        hdn = jnp.square(jax.nn.relu(rms_norm(x, g_mlp[l]) @ w_up[l]))
        x = x + hdn @ w_down[l]
        ple_gate = jax.nn.sigmoid(rms_norm(x, g_ple[l]) @ w_ple_gate[l])
        x = x + ple_gate * (p[l] @ w_ple_proj[l])
    return rms_norm(x, g_final)
```

```python
import functools
import math

import jax
import jax.numpy as jnp
from jax import lax
from jax.experimental import pallas as pl
from jax.experimental.pallas import tpu as pltpu

F32 = jnp.float32
BF16 = jnp.bfloat16

D_MODEL = 2048
N_HEADS = 16
QK_NOPE = 128
QK_ROPE = 64
V_DIM = 128
Q_RANK = 512
KV_RANK = 512
ROPE_THETA = 10000.0
LRU_HEADS = 16
LRU_BLOCK = 128
CONV_WIDTH = 4
LRU_C = 8.0
D_FF = 4 * D_MODEL
PLE_DIM = 256
EPS = 1e-6

HALF = QK_ROPE // 2
QK_PAD = 256
LANE = 128
TOK = 512
VMEM_LIMIT = 56 * 1024 * 1024

Q_SCALE = (QK_NOPE + QK_ROPE) ** -0.5 * math.log2(math.e)


def _params(*sem):
    return pltpu.CompilerParams(dimension_semantics=sem, vmem_limit_bytes=VMEM_LIMIT)


def _rms(x, g):
    ms = jnp.mean(x * x, axis=-1, keepdims=True)
    return x * lax.rsqrt(ms + EPS) * g


def _dot(a, b):
    return jnp.dot(a, b, preferred_element_type=F32)


def _dot_nt(a, b):
    return lax.dot_general(a, b, (((1,), (1,)), ((), ())), preferred_element_type=F32)


def _rope_kernel(pos_ref, invf_ref, ct_ref, st_ref, c_ref, s_ref):
    ang = invf_ref[...] * pos_ref[...].astype(F32)
    row = lax.broadcasted_iota(jnp.int32, ang.shape, 0)
    sn = jnp.sin(ang)
    c = jnp.where(row < QK_ROPE, jnp.cos(ang), 0.0)
    s = jnp.where(row < HALF, -sn, jnp.where(row < QK_ROPE, sn, 0.0))
    ct_ref[...] = c
    st_ref[...] = s
    c_ref[...] = c.T
    s_ref[...] = s.T


def _rope_tables(positions):
    t = positions.size
    tb = 2048
    inv_freq = jnp.power(jnp.float32(ROPE_THETA), -jnp.arange(HALF, dtype=F32) * (2.0 / QK_ROPE))
    invf = jnp.tile(inv_freq, LANE // HALF).reshape(LANE, 1)
    pos = positions.reshape(1, t)
    return pl.pallas_call(
        _rope_kernel,
        grid=(t // tb,),
        in_specs=[pl.BlockSpec((1, tb), lambda i: (0, i)),
                  pl.BlockSpec((LANE, 1), lambda i: (0, 0))],
        out_specs=[pl.BlockSpec((LANE, tb), lambda i: (0, i)),
                   pl.BlockSpec((LANE, tb), lambda i: (0, i)),
                   pl.BlockSpec((tb, LANE), lambda i: (i, 0)),
                   pl.BlockSpec((tb, LANE), lambda i: (i, 0))],
        out_shape=[jax.ShapeDtypeStruct((LANE, t), F32), jax.ShapeDtypeStruct((LANE, t), F32),
                   jax.ShapeDtypeStruct((t, LANE), F32), jax.ShapeDtypeStruct((t, LANE), F32)],
        compiler_params=_params("parallel"),
        name="rope_tables",
    )(pos, invf)


def _mla_down_kernel(x_ref, g_ref, w_ref, gq_ref, gkv_ref, c_ref, s_ref, cq_ref, ckv_ref, kpe_ref):
    u = _rms(x_ref[...], g_ref[...]).astype(BF16)
    z = _dot(u, w_ref[...])
    cq_ref[...] = _rms(z[:, :Q_RANK], gq_ref[...]).astype(BF16)
    ckv_ref[...] = _rms(z[:, Q_RANK:Q_RANK + KV_RANK], gkv_ref[...]).astype(BF16)
    o = Q_RANK + KV_RANK
    kpe = z[:, o:o + LANE] * c_ref[...] + z[:, o + LANE:o + 2 * LANE] * s_ref[...]
    kpe_ref[...] = kpe.astype(BF16)


def _mla_down(x2, g_mix, w_a, g_q, g_kv, c_tab, s_tab):
    t = x2.shape[0]
    tm = 512
    na = w_a.shape[1]
    row = lambda i: (i, 0)
    fix = lambda i: (0, 0)
    return pl.pallas_call(
        _mla_down_kernel,
        grid=(t // tm,),
        in_specs=[pl.BlockSpec((tm, D_MODEL), row),
                  pl.BlockSpec((1, D_MODEL), fix),
                  pl.BlockSpec((D_MODEL, na), fix),
                  pl.BlockSpec((1, Q_RANK), fix),
                  pl.BlockSpec((1, KV_RANK), fix),
                  pl.BlockSpec((tm, LANE), row),
                  pl.BlockSpec((tm, LANE), row)],
        out_specs=[pl.BlockSpec((tm, Q_RANK), row),
                   pl.BlockSpec((tm, KV_RANK), row),
                   pl.BlockSpec((tm, LANE), row)],
        out_shape=[jax.ShapeDtypeStruct((t, Q_RANK), BF16),
                   jax.ShapeDtypeStruct((t, KV_RANK), BF16),
                   jax.ShapeDtypeStruct((t, LANE), BF16)],
        compiler_params=_params("parallel"),
        name="mla_down",
    )(x2, g_mix, w_a, g_q, g_kv, c_tab, s_tab)


def _mla_up_kernel(cq_ref, ckv_ref, ct_ref, st_ref, wq_ref, wk_ref, wv_ref, q_ref, k_ref, v_ref):
    cq = cq_ref[...]
    ckv = ckv_ref[...]
    k_ref[...] = _dot(ckv, wk_ref[...]).astype(BF16)
    v_ref[...] = _dot_nt(wv_ref[...], ckv).astype(BF16)
    ct = ct_ref[...]
    st = st_ref[...]
    zero = jnp.zeros((QK_PAD - QK_NOPE - QK_ROPE, cq.shape[0]), BF16)
    for h in range(N_HEADS):
        qh = _dot_nt(wq_ref[h], cq)
        q_ref[h, 0:QK_NOPE, :] = (qh[0:QK_NOPE] * Q_SCALE).astype(BF16)
        a = qh[QK_NOPE:QK_NOPE + QK_ROPE]
        b = qh[QK_NOPE + QK_ROPE:QK_NOPE + 2 * QK_ROPE]
        q_ref[h, QK_NOPE:QK_NOPE + QK_ROPE, :] = ((a * ct + b * st) * Q_SCALE).astype(BF16)
        q_ref[h, QK_NOPE + QK_ROPE:QK_PAD, :] = zero


def _mla_up(cq, ckv, ct_tab, st_tab, wq_t, wk, wv_t):
    t = cq.shape[0]
    tm = TOK
    nb = t // tm
    row = lambda i: (i, 0)
    return pl.pallas_call(
        _mla_up_kernel,
        grid=(nb,),
        in_specs=[pl.BlockSpec((tm, Q_RANK), row),
                  pl.BlockSpec((tm, KV_RANK), row),
                  pl.BlockSpec((QK_ROPE, tm), lambda i: (0, i)),
                  pl.BlockSpec((QK_ROPE, tm), lambda i: (0, i)),
                  pl.BlockSpec((N_HEADS, QK_PAD, Q_RANK), lambda i: (0, 0, 0)),
                  pl.BlockSpec((KV_RANK, N_HEADS * QK_NOPE), lambda i: (0, 0)),
                  pl.BlockSpec((N_HEADS * V_DIM, KV_RANK), lambda i: (0, 0))],
        out_specs=[pl.BlockSpec((None, N_HEADS, QK_PAD, tm), lambda i: (i, 0, 0, 0)),
                   pl.BlockSpec((tm, N_HEADS * QK_NOPE), row),
                   pl.BlockSpec((None, N_HEADS * V_DIM, tm), lambda i: (i, 0, 0))],
        out_shape=[jax.ShapeDtypeStruct((nb, N_HEADS, QK_PAD, tm), BF16),
                   jax.ShapeDtypeStruct((t, N_HEADS * QK_NOPE), BF16),
                   jax.ShapeDtypeStruct((nb, N_HEADS * V_DIM, tm), BF16)],
        compiler_params=_params("parallel"),
        name="mla_up",
    )(cq, ckv, ct_tab, st_tab, wq_t, wk, wv_t)


def _attn_kernel(q_ref, k_ref, kpe_ref, v_ref, o_ref, kcat_ref):
    qi = pl.program_id(2)

    @pl.when(qi == 0)
    def _():
        kcat_ref[:, 0:QK_NOPE] = k_ref[...]
        kcat_ref[:, QK_NOPE:QK_PAD] = kpe_ref[...]

    q = q_ref[...]

    def step(j, carry, masked):
        m, l, acc = carry
        kb = kcat_ref[pl.ds(pl.multiple_of(j * TOK, TOK), TOK), :]
        s = _dot(kb, q)
        if masked:
            kpos = lax.broadcasted_iota(jnp.int32, s.shape, 0)
            qpos = lax.broadcasted_iota(jnp.int32, s.shape, 1)
            s = jnp.where(kpos <= qpos, s, -jnp.inf)
        m_new = jnp.maximum(m, jnp.max(s, axis=0, keepdims=True))
        alpha = jnp.exp2(m - m_new)
        p = jnp.exp2(s - m_new)
        l = alpha * l + jnp.sum(p, axis=0, keepdims=True)
        acc = alpha * acc + _dot(v_ref[j], p.astype(BF16))
        return m_new, l, acc

    init = (jnp.full((1, TOK), -jnp.inf, F32), jnp.zeros((1, TOK), F32), jnp.zeros((V_DIM, TOK), F32))
    carry = lax.fori_loop(0, qi, functools.partial(step, masked=False), init)
    _, l, acc = step(qi, carry, masked=True)
    o_ref[...] = (acc / l).T.astype(BF16)


def _attention(q_t, k, kpe, v_t, batch, seq):
    t = k.shape[0]
    nq = seq // TOK
    return pl.pallas_call(
        _attn_kernel,
        grid=(batch, N_HEADS, nq),
        in_specs=[pl.BlockSpec((None, None, QK_PAD, TOK), lambda b, h, i: (b * nq + i, h, 0, 0)),
                  pl.BlockSpec((seq, QK_NOPE), lambda b, h, i: (b, h)),
                  pl.BlockSpec((seq, LANE), lambda b, h, i: (b, 0)),
                  pl.BlockSpec((nq, V_DIM, TOK), lambda b, h, i: (b, h, 0))],
        out_specs=pl.BlockSpec((TOK, V_DIM), lambda b, h, i: (b * nq + i, h)),
        out_shape=jax.ShapeDtypeStruct((t, N_HEADS * V_DIM), BF16),
        scratch_shapes=[pltpu.VMEM((seq, QK_PAD), BF16)],
        compiler_params=_params("parallel", "parallel", "arbitrary"),
        name="mla_attention",
    )(q_t, k, kpe, v_t)


def _in_proj_kernel(x_ref, g_ref, w_ref, o_ref, u_ref):
    @pl.when(pl.program_id(1) == 0)
    def _():
        u_ref[...] = _rms(x_ref[...], g_ref[...]).astype(BF16)

    o_ref[...] = _dot(u_ref[...], w_ref[...]).astype(BF16)


def _in_proj(x2, g_mix, w_b):
    t = x2.shape[0]
    n = w_b.shape[1]
    tm, tn = 1024, 1024
    return pl.pallas_call(
        _in_proj_kernel,
        grid=(t // tm, n // tn),
        in_specs=[pl.BlockSpec((tm, D_MODEL), lambda i, j: (i, 0)),
                  pl.BlockSpec((1, D_MODEL), lambda i, j: (0, 0)),
                  pl.BlockSpec((D_MODEL, tn), lambda i, j: (0, j))],
        out_specs=pl.BlockSpec((tm, tn), lambda i, j: (i, j)),
        out_shape=jax.ShapeDtypeStruct((t, n), BF16),
        scratch_shapes=[pltpu.VMEM((tm, D_MODEL), BF16)],
        compiler_params=_params("parallel", "arbitrary"),
        name="in_proj",
    )(x2, g_mix, w_b)


def _softplus(z):
    return jnp.maximum(z, 0.0) + jnp.log1p(jnp.exp(-jnp.abs(z)))


def _rglru_kernel(xb_ref, yb_ref, cw_ref, cb_ref, wax_ref, ba_ref, bx_ref, lam_ref, o_ref,
                  xpad_ref, h_ref):
    ts, tc = xb_ref.shape
    pad = 8

    @pl.when(pl.program_id(2) == 0)
    def _():
        xpad_ref[0:pad, :] = jnp.zeros((pad, tc), F32)
        h_ref[...] = jnp.zeros_like(h_ref)

    xpad_ref[pad:pad + ts, :] = xb_ref[...].astype(F32)
    cw = cw_ref[...]
    xc = cb_ref[...] + cw[CONV_WIDTH - 1:CONV_WIDTH, :] * xpad_ref[pad:pad + ts, :]
    for k in range(1, CONV_WIDTH):
        w_row = cw[CONV_WIDTH - 1 - k:CONV_WIDTH - k, :]
        xc = xc + w_row * xpad_ref[pad - k:pad - k + ts, :]
    xpad_ref[0:pad, :] = xpad_ref[ts:ts + pad, :]

    xcb = xc.astype(BF16)
    gates = [_dot(xcb[:, h * LRU_BLOCK:(h + 1) * LRU_BLOCK], wax_ref[h]) for h in range(tc // LRU_BLOCK)]
    ga = jnp.concatenate([g[:, :LRU_BLOCK] for g in gates], axis=1)
    gx = jnp.concatenate([g[:, LRU_BLOCK:] for g in gates], axis=1)
    r = jax.nn.sigmoid(ga + ba_ref[...])
    i = jax.nn.sigmoid(gx + bx_ref[...])
    log_a = (-LRU_C) * r * _softplus(-lam_ref[...])
    a = jnp.exp(log_a)
    b = jnp.sqrt(-jnp.tanh(log_a) * (a * a + 1.0)) * (i * xc)

    row = lax.broadcasted_iota(jnp.int32, (ts, tc), 0)
    b = jnp.where(row == 0, a * h_ref[...] + b, b)
    d = 1
    while d < ts:
        keep = row >= d
        b = b + a * jnp.where(keep, pltpu.roll(b, d, 0), 0.0)
        if 2 * d < ts:
            a = a * jnp.where(keep, pltpu.roll(a, d, 0), 1.0)
        d *= 2
    h_ref[...] = b[ts - 1:ts, :]
    o_ref[...] = (jax.nn.gelu(yb_ref[...].astype(F32), approximate=True) * b).astype(BF16)


def _rglru(zb, conv_w, conv_b, w_ax, b_a, b_x, lam, batch, seq):
    t = zb.shape[0]
    ts, tc = 256, 512
    ns = seq // ts
    ncb = D_MODEL // tc
    chan = lambda b, c, s: (0, c)
    return pl.pallas_call(
        _rglru_kernel,
        grid=(batch, ncb, ns),
        in_specs=[pl.BlockSpec((ts, tc), lambda b, c, s: (b * ns + s, c)),
                  pl.BlockSpec((ts, tc), lambda b, c, s: (b * ns + s, ncb + c)),
                  pl.BlockSpec((CONV_WIDTH, tc), chan),
                  pl.BlockSpec((1, tc), chan),
                  pl.BlockSpec((tc // LRU_BLOCK, LRU_BLOCK, 2 * LRU_BLOCK), lambda b, c, s: (c, 0, 0)),
                  pl.BlockSpec((1, tc), chan),
                  pl.BlockSpec((1, tc), chan),
                  pl.BlockSpec((1, tc), chan)],
        out_specs=pl.BlockSpec((ts, tc), lambda b, c, s: (b * ns + s, c)),
        out_shape=jax.ShapeDtypeStruct((t, D_MODEL), BF16),
        scratch_shapes=[pltpu.VMEM((ts + 8, tc), F32), pltpu.VMEM((1, tc), F32)],
        compiler_params=_params("parallel", "parallel", "arbitrary"),
        name="rglru",
    )(zb, zb, conv_w, conv_b, w_ax, b_a, b_x, lam)


def _out_proj_kernel(attn_ref, rec_ref, ga_ref, gr_ref, x_ref, w_ref, o_ref):
    merged = (jax.nn.sigmoid(ga_ref[...].astype(F32)) * attn_ref[...].astype(F32)
              + jax.nn.sigmoid(gr_ref[...].astype(F32)) * rec_ref[...].astype(F32))
    o_ref[...] = x_ref[...] + _dot(merged.astype(BF16), w_ref[...])


def _out_proj(attn, rec, zb, x2, w_out):
    t = x2.shape[0]
    tm = 512
    row = lambda i: (i, 0)
    return pl.pallas_call(
        _out_proj_kernel,
        grid=(t // tm,),
        in_specs=[pl.BlockSpec((tm, D_MODEL), row),
                  pl.BlockSpec((tm, D_MODEL), row),
                  pl.BlockSpec((tm, D_MODEL), lambda i: (i, 2)),
                  pl.BlockSpec((tm, D_MODEL), lambda i: (i, 3)),
                  pl.BlockSpec((tm, D_MODEL), row),
                  pl.BlockSpec((D_MODEL, D_MODEL), lambda i: (0, 0))],
        out_specs=pl.BlockSpec((tm, D_MODEL), row),
        out_shape=jax.ShapeDtypeStruct((t, D_MODEL), F32),
        compiler_params=_params("parallel"),
        name="out_proj",
    )(attn, rec, zb, zb, x2, w_out)


def _mlp_kernel(x_ref, g_ref, wu_ref, wd_ref, o_ref, u_ref):
    @pl.when(pl.program_id(1) == 0)
    def _():
        x = x_ref[...]
        u_ref[...] = _rms(x, g_ref[...]).astype(BF16)
        o_ref[...] = x

    hid = jnp.square(jnp.maximum(_dot(u_ref[...], wu_ref[...]), 0.0)).astype(BF16)
    o_ref[...] += _dot(hid, wd_ref[...])


def _mlp(x2, g_mlp, w_up, w_down):
    t = x2.shape[0]
    tm, tf = 512, 1024
    return pl.pallas_call(
        _mlp_kernel,
        grid=(t // tm, D_FF // tf),
        in_specs=[pl.BlockSpec((tm, D_MODEL), lambda i, j: (i, 0)),
                  pl.BlockSpec((1, D_MODEL), lambda i, j: (0, 0)),
                  pl.BlockSpec((D_MODEL, tf), lambda i, j: (0, j)),
                  pl.BlockSpec((tf, D_MODEL), lambda i, j: (j, 0))],
        out_specs=pl.BlockSpec((tm, D_MODEL), lambda i, j: (i, 0)),
        out_shape=jax.ShapeDtypeStruct((t, D_MODEL), F32),
        scratch_shapes=[pltpu.VMEM((tm, D_MODEL), BF16)],
        compiler_params=_params("parallel", "arbitrary"),
        name="mlp",
    )(x2, g_mlp, w_up, w_down)


def _ple_kernel(x_ref, g_ref, wg_ref, p_ref, wp_ref, o_ref):
    x = x_ref[...]
    gate = jax.nn.sigmoid(_dot(_rms(x, g_ref[...]).astype(BF16), wg_ref[...]))
    emb = _dot(p_ref[...].astype(BF16), wp_ref[...])
    o_ref[...] = x + gate * emb


def _ple(x2, g_ple, w_gate, p2, w_proj):
    t = x2.shape[0]
    tm = 512
    row = lambda i: (i, 0)
    fix = lambda i: (0, 0)
    return pl.pallas_call(
        _ple_kernel,
        grid=(t // tm,),
        in_specs=[pl.BlockSpec((tm, D_MODEL), row),
                  pl.BlockSpec((1, D_MODEL), fix),
                  pl.BlockSpec((D_MODEL, D_MODEL), fix),
                  pl.BlockSpec((tm, PLE_DIM), row),
                  pl.BlockSpec((PLE_DIM, D_MODEL), fix)],
        out_specs=pl.BlockSpec((tm, D_MODEL), row),
        out_shape=jax.ShapeDtypeStruct((t, D_MODEL), F32),
        compiler_params=_params("parallel"),
        name="ple",
    )(x2, g_ple, w_gate, p2, w_proj)


def _final_norm_kernel(x_ref, g_ref, o_ref):
    o_ref[...] = _rms(x_ref[...], g_ref[...])


def _final_norm(x2, g):
    t = x2.shape[0]
    tm = 1024
    return pl.pallas_call(
        _final_norm_kernel,
        grid=(t // tm,),
        in_specs=[pl.BlockSpec((tm, D_MODEL), lambda i: (i, 0)),
                  pl.BlockSpec((1, D_MODEL), lambda i: (0, 0))],
        out_specs=pl.BlockSpec((tm, D_MODEL), lambda i: (i, 0)),
        out_shape=jax.ShapeDtypeStruct((t, D_MODEL), F32),
        compiler_params=_params("parallel"),
        name="final_norm",
    )(x2, g)


def _prep_in_proj(w_in):
    o = Q_RANK + KV_RANK
    k1 = w_in[:, o:o + HALF]
    k2 = w_in[:, o + HALF:o + QK_ROPE]
    zpad = jnp.zeros((D_MODEL, LANE - QK_ROPE), w_in.dtype)
    w_a = jnp.concatenate([w_in[:, :o], k1, k2, zpad, k2, k1, zpad], axis=1)
    w_b = w_in[:, o + QK_ROPE:]
    return w_a.astype(BF16), w_b.astype(BF16)


def _prep_q(w_qb):
    w = w_qb.reshape(Q_RANK, N_HEADS, QK_NOPE + QK_ROPE)
    nope = w[:, :, :QK_NOPE]
    q1 = w[:, :, QK_NOPE:QK_NOPE + HALF]
    q2 = w[:, :, QK_NOPE + HALF:]
    full = jnp.concatenate([nope, q1, q2, q2, q1], axis=2)
    return jnp.transpose(full, (1, 2, 0)).astype(BF16)


def _prep_kv(w_kvb):
    w = w_kvb.reshape(KV_RANK, N_HEADS, QK_NOPE + V_DIM)
    wk = w[:, :, :QK_NOPE].reshape(KV_RANK, N_HEADS * QK_NOPE)
    wv_t = w[:, :, QK_NOPE:].reshape(KV_RANK, N_HEADS * V_DIM).T
    return wk.astype(BF16), wv_t.astype(BF16)


def kernel(x, p, positions, g_mix, w_in, g_q, w_qb, g_kv, w_kvb, conv_w, conv_b, w_a, b_a, w_x, b_x,
           lru_lambda, w_out, g_mlp, w_up, w_down, g_ple, w_ple_gate, w_ple_proj, g_final):
    batch, seq, d = x.shape
    depth = w_in.shape[0]
    t = batch * seq
    assert d == D_MODEL and seq % TOK == 0

    ct_tab, st_tab, c_tab, s_tab = _rope_tables(positions)
    x2 = x.reshape(t, d)
    row = lambda v: v.reshape(1, -1)

    for l in range(depth):
        w_in_a, w_in_b = _prep_in_proj(w_in[l])
        wq_t = _prep_q(w_qb[l])
        wk, wv_t = _prep_kv(w_kvb[l])
        w_ax = jnp.concatenate([w_a[l], w_x[l]], axis=-1).astype(BF16)

        cq, ckv, kpe = _mla_down(x2, row(g_mix[l]), w_in_a, row(g_q[l]), row(g_kv[l]), c_tab, s_tab)
        q_t, k, v_t = _mla_up(cq, ckv, ct_tab, st_tab, wq_t, wk, wv_t)
        attn = _attention(q_t, k, kpe, v_t, batch, seq)

        zb = _in_proj(x2, row(g_mix[l]), w_in_b)
        rec = _rglru(zb, conv_w[l], row(conv_b[l]), w_ax, row(b_a[l]), row(b_x[l]),
                     row(lru_lambda[l]), batch, seq)

        x2 = _out_proj(attn, rec, zb, x2, w_out[l].astype(BF16))
        x2 = _mlp(x2, row(g_mlp[l]), w_up[l].astype(BF16), w_down[l].astype(BF16))
        x2 = _ple(x2, row(g_ple[l]), w_ple_gate[l].astype(BF16), p[l].reshape(t, PLE_DIM),
                  w_ple_proj[l].astype(BF16))

    return _final_norm(x2, row(g_final)).reshape(batch, seq, d)
```

```python
import functools
import math

import jax
import jax.numpy as jnp
from jax import lax
from jax.experimental import pallas as pl
from jax.experimental.pallas import tpu as pltpu

F32 = jnp.float32
BF16 = jnp.bfloat16

D_MODEL = 2048
N_HEADS = 16
QK_NOPE = 128
QK_ROPE = 64
V_DIM = 128
Q_RANK = 512
KV_RANK = 512
ROPE_THETA = 10000.0
LRU_HEADS = 16
LRU_BLOCK = 128
CONV_WIDTH = 4
LRU_C = 8.0
D_FF = 4 * D_MODEL
PLE_DIM = 256
EPS = 1e-6

HALF = QK_ROPE // 2
QK_PAD = 256
LANE = 128
TOK = 512
VMEM_LIMIT = 56 * 1024 * 1024

Q_SCALE = (QK_NOPE + QK_ROPE) ** -0.5 * math.log2(math.e)


def _params(*sem):
    return pltpu.CompilerParams(dimension_semantics=sem, vmem_limit_bytes=VMEM_LIMIT)


def _rms(x, g):
    ms = jnp.mean(x * x, axis=-1, keepdims=True)
    return x * lax.rsqrt(ms + EPS) * g


def _dot(a, b):
    return jnp.dot(a, b, preferred_element_type=F32)


def _dot_nt(a, b):
    return lax.dot_general(a, b, (((1,), (1,)), ((), ())), preferred_element_type=F32)


def _rope_kernel(pos_ref, invf_ref, ct_ref, st_ref, c_ref, s_ref):
    ang = invf_ref[...] * pos_ref[...].astype(F32)
    row = lax.broadcasted_iota(jnp.int32, ang.shape, 0)
    sn = jnp.sin(ang)
    c = jnp.where(row < QK_ROPE, jnp.cos(ang), 0.0)
    s = jnp.where(row < HALF, -sn, jnp.where(row < QK_ROPE, sn, 0.0))
    ct_ref[...] = c
    st_ref[...] = s
    c_ref[...] = c.T
    s_ref[...] = s.T


def _rope_tables(positions):
    t = positions.size
    tb = 2048
    inv_freq = jnp.power(jnp.float32(ROPE_THETA), -jnp.arange(HALF, dtype=F32) * (2.0 / QK_ROPE))
    invf = jnp.tile(inv_freq, LANE // HALF).reshape(LANE, 1)
    pos = positions.reshape(1, t)
    return pl.pallas_call(
        _rope_kernel,
        grid=(t // tb,),
        in_specs=[pl.BlockSpec((1, tb), lambda i: (0, i)),
                  pl.BlockSpec((LANE, 1), lambda i: (0, 0))],
        out_specs=[pl.BlockSpec((LANE, tb), lambda i: (0, i)),
                   pl.BlockSpec((LANE, tb), lambda i: (0, i)),
                   pl.BlockSpec((tb, LANE), lambda i: (i, 0)),
                   pl.BlockSpec((tb, LANE), lambda i: (i, 0))],
        out_shape=[jax.ShapeDtypeStruct((LANE, t), F32), jax.ShapeDtypeStruct((LANE, t), F32),
                   jax.ShapeDtypeStruct((t, LANE), F32), jax.ShapeDtypeStruct((t, LANE), F32)],
        compiler_params=_params("parallel"),
        name="rope_tables",
    )(pos, invf)


def _mla_down_kernel(x_ref, g_ref, w_ref, gq_ref, gkv_ref, c_ref, s_ref, cq_ref, ckv_ref, kpe_ref):
    u = _rms(x_ref[...], g_ref[...]).astype(BF16)
    z = _dot(u, w_ref[...])
    cq_ref[...] = _rms(z[:, :Q_RANK], gq_ref[...]).astype(BF16)
    ckv_ref[...] = _rms(z[:, Q_RANK:Q_RANK + KV_RANK], gkv_ref[...]).astype(BF16)
    o = Q_RANK + KV_RANK
    kpe = z[:, o:o + LANE] * c_ref[...] + z[:, o + LANE:o + 2 * LANE] * s_ref[...]
    kpe_ref[...] = kpe.astype(BF16)


def _mla_down(x2, g_mix, w_a, g_q, g_kv, c_tab, s_tab):
    t = x2.shape[0]
    tm = 512
    na = w_a.shape[1]
    row = lambda i: (i, 0)
    fix = lambda i: (0, 0)
    return pl.pallas_call(
        _mla_down_kernel,
        grid=(t // tm,),
        in_specs=[pl.BlockSpec((tm, D_MODEL), row),
                  pl.BlockSpec((1, D_MODEL), fix),
                  pl.BlockSpec((D_MODEL, na), fix),
                  pl.BlockSpec((1, Q_RANK), fix),
                  pl.BlockSpec((1, KV_RANK), fix),
                  pl.BlockSpec((tm, LANE), row),
                  pl.BlockSpec((tm, LANE), row)],
        out_specs=[pl.BlockSpec((tm, Q_RANK), row),
                   pl.BlockSpec((tm, KV_RANK), row),
                   pl.BlockSpec((tm, LANE), row)],
        out_shape=[jax.ShapeDtypeStruct((t, Q_RANK), BF16),
                   jax.ShapeDtypeStruct((t, KV_RANK), BF16),
                   jax.ShapeDtypeStruct((t, LANE), BF16)],
        compiler_params=_params("parallel"),
        name="mla_down",
    )(x2, g_mix, w_a, g_q, g_kv, c_tab, s_tab)


def _mla_up_kernel(cq_ref, ckv_ref, ct_ref, st_ref, wq_ref, wk_ref, wv_ref, q_ref, k_ref, v_ref):
    cq = cq_ref[...]
    ckv = ckv_ref[...]
    k_ref[...] = _dot(ckv, wk_ref[...]).astype(BF16)
    v_ref[...] = _dot_nt(wv_ref[...], ckv).astype(BF16)
    ct = ct_ref[...]
    st = st_ref[...]
    zero = jnp.zeros((QK_PAD - QK_NOPE - QK_ROPE, cq.shape[0]), BF16)
    for h in range(N_HEADS):
        qh = _dot_nt(wq_ref[h], cq)
        q_ref[h, 0:QK_NOPE, :] = (qh[0:QK_NOPE] * Q_SCALE).astype(BF16)
        a = qh[QK_NOPE:QK_NOPE + QK_ROPE]
        b = qh[QK_NOPE + QK_ROPE:QK_NOPE + 2 * QK_ROPE]
        q_ref[h, QK_NOPE:QK_NOPE + QK_ROPE, :] = ((a * ct + b * st) * Q_SCALE).astype(BF16)
        q_ref[h, QK_NOPE + QK_ROPE:QK_PAD, :] = zero


def _mla_up(cq, ckv, ct_tab, st_tab, wq_t, wk, wv_t):
    t = cq.shape[0]
    tm = TOK
    nb = t // tm
    row = lambda i: (i, 0)
    return pl.pallas_call(
        _mla_up_kernel,
        grid=(nb,),
        in_specs=[pl.BlockSpec((tm, Q_RANK), row),
                  pl.BlockSpec((tm, KV_RANK), row),
                  pl.BlockSpec((QK_ROPE, tm), lambda i: (0, i)),
                  pl.BlockSpec((QK_ROPE, tm), lambda i: (0, i)),
                  pl.BlockSpec((N_HEADS, QK_PAD, Q_RANK), lambda i: (0, 0, 0)),
                  pl.BlockSpec((KV_RANK, N_HEADS * QK_NOPE), lambda i: (0, 0)),
                  pl.BlockSpec((N_HEADS * V_DIM, KV_RANK), lambda i: (0, 0))],
        out_specs=[pl.BlockSpec((None, N_HEADS, QK_PAD, tm), lambda i: (i, 0, 0, 0)),
                   pl.BlockSpec((tm, N_HEADS * QK_NOPE), row),
                   pl.BlockSpec((None, N_HEADS * V_DIM, tm), lambda i: (i, 0, 0))],
        out_shape=[jax.ShapeDtypeStruct((nb, N_HEADS, QK_PAD, tm), BF16),
                   jax.ShapeDtypeStruct((t, N_HEADS * QK_NOPE), BF16),
                   jax.ShapeDtypeStruct((nb, N_HEADS * V_DIM, tm), BF16)],
        compiler_params=_params("parallel"),
        name="mla_up",
    )(cq, ckv, ct_tab, st_tab, wq_t, wk, wv_t)


KV_HALF = TOK // 2
V_AUG = V_DIM + 16


HPS = 2


def _attn_kernel(q_ref, k_ref, kpe_ref, v_ref, o_ref,
                 kcat_ref, vaug_ref, rc_ref, sa_ref, sb_ref, mxa_ref, mxb_ref, m_ref, acc_ref):
    qi = pl.program_id(2)
    nq = v_ref.shape[0]
    heads = range(HPS)

    @pl.when(qi == 0)
    def _():
        ones = jnp.ones((V_AUG - V_DIM, KV_HALF), BF16)
        for g in heads:
            kcat_ref[g, :, 0:QK_NOPE] = k_ref[:, g * QK_NOPE:(g + 1) * QK_NOPE]
            kcat_ref[g, :, QK_NOPE:QK_PAD] = kpe_ref[...]
            for j in range(nq):
                for hf in range(2):
                    vaug_ref[g, 2 * j + hf, 0:V_DIM, :] = v_ref[j, g * V_DIM:(g + 1) * V_DIM,
                                                                hf * KV_HALF:(hf + 1) * KV_HALF]
                    vaug_ref[g, 2 * j + hf, V_DIM:V_AUG, :] = ones
        rc_ref[...] = (lax.broadcasted_iota(jnp.int32, rc_ref.shape, 0)
                       - lax.broadcasted_iota(jnp.int32, rc_ref.shape, 1))

    q_start = qi * TOK

    def scores(half, s_ref, mx_ref, masked):
        start = pl.multiple_of(half * KV_HALF, KV_HALF)
        for g in heads:
            s = _dot(kcat_ref[g, pl.ds(start, KV_HALF), :], q_ref[g])
            if masked:
                s = jnp.where(rc_ref[...] <= q_start - half * KV_HALF, s, -jnp.inf)
            s_ref[g] = s
            mx_ref[g] = jnp.max(s, axis=0, keepdims=True)

    def update(half, s_ref, mx_ref):
        for g in heads:
            m_old = m_ref[g]
            m_new = jnp.maximum(m_old, mx_ref[g])
            alpha = jnp.exp2(m_old - m_new)
            p = jnp.exp2((s_ref[g] - m_new).astype(BF16))
            acc_ref[g] = alpha * acc_ref[g] + _dot(vaug_ref[g, half], p)
            m_ref[g] = m_new

    m_ref[...] = jnp.full(m_ref.shape, -jnp.inf, F32)
    acc_ref[...] = jnp.zeros(acc_ref.shape, F32)
    scores(0, sa_ref, mxa_ref, masked=True)

    def body(j, carry):
        scores(2 * j + 1, sb_ref, mxb_ref, masked=False)
        update(2 * j, sa_ref, mxa_ref)
        scores(2 * j + 2, sa_ref, mxa_ref, masked=True)
        update(2 * j + 1, sb_ref, mxb_ref)
        return carry

    lax.fori_loop(0, qi, body, 0)
    scores(2 * qi + 1, sb_ref, mxb_ref, masked=True)
    update(2 * qi, sa_ref, mxa_ref)
    update(2 * qi + 1, sb_ref, mxb_ref)
    for g in heads:
        inv_l = 1.0 / acc_ref[g, V_DIM:V_DIM + 1, :]
        o_ref[:, g * V_DIM:(g + 1) * V_DIM] = (acc_ref[g, 0:V_DIM, :] * inv_l).T.astype(BF16)


def _attention(q_t, k, kpe, v_t, batch, seq):
    t = k.shape[0]
    nq = seq // TOK
    return pl.pallas_call(
        _attn_kernel,
        grid=(batch, N_HEADS // HPS, nq),
        in_specs=[pl.BlockSpec((None, HPS, QK_PAD, TOK), lambda b, h, i: (b * nq + i, h, 0, 0)),
                  pl.BlockSpec((seq, HPS * QK_NOPE), lambda b, h, i: (b, h)),
                  pl.BlockSpec((seq, LANE), lambda b, h, i: (b, 0)),
                  pl.BlockSpec((nq, HPS * V_DIM, TOK), lambda b, h, i: (b, h, 0))],
        out_specs=pl.BlockSpec((TOK, HPS * V_DIM), lambda b, h, i: (b * nq + i, h)),
        out_shape=jax.ShapeDtypeStruct((t, N_HEADS * V_DIM), BF16),
        scratch_shapes=[pltpu.VMEM((HPS, seq, QK_PAD), BF16),
                        pltpu.VMEM((HPS, 2 * nq, V_AUG, KV_HALF), BF16),
                        pltpu.VMEM((KV_HALF, TOK), jnp.int32),
                        pltpu.VMEM((HPS, KV_HALF, TOK), F32),
                        pltpu.VMEM((HPS, KV_HALF, TOK), F32),
                        pltpu.VMEM((HPS, 1, TOK), F32),
                        pltpu.VMEM((HPS, 1, TOK), F32),
                        pltpu.VMEM((HPS, 1, TOK), F32),
                        pltpu.VMEM((HPS, V_AUG, TOK), F32)],
        compiler_params=_params("parallel", "parallel", "arbitrary"),
        name="mla_attention",
    )(q_t, k, kpe, v_t)


def _in_proj_kernel(x_ref, g_ref, w_ref, o_ref, u_ref):
    @pl.when(pl.program_id(1) == 0)
    def _():
        u_ref[...] = _rms(x_ref[...], g_ref[...]).astype(BF16)

    o_ref[...] = _dot(u_ref[...], w_ref[...]).astype(BF16)


def _in_proj(x2, g_mix, w_b):
    t = x2.shape[0]
    n = w_b.shape[1]
    tm, tn = 1024, 1024
    return pl.pallas_call(
        _in_proj_kernel,
        grid=(t // tm, n // tn),
        in_specs=[pl.BlockSpec((tm, D_MODEL), lambda i, j: (i, 0)),
                  pl.BlockSpec((1, D_MODEL), lambda i, j: (0, 0)),
                  pl.BlockSpec((D_MODEL, tn), lambda i, j: (0, j))],
        out_specs=pl.BlockSpec((tm, tn), lambda i, j: (i, j)),
        out_shape=jax.ShapeDtypeStruct((t, n), BF16),
        scratch_shapes=[pltpu.VMEM((tm, D_MODEL), BF16)],
        compiler_params=_params("parallel", "arbitrary"),
        name="in_proj",
    )(x2, g_mix, w_b)


def _softplus(z):
    return jnp.maximum(z, 0.0) + jnp.log1p(jnp.exp(-jnp.abs(z)))


def _rglru_kernel(xb_ref, yb_ref, cw_ref, cb_ref, wax_ref, ba_ref, bx_ref, lam_ref, o_ref,
                  xpad_ref, h_ref):
    ts, tc = xb_ref.shape
    pad = 8

    @pl.when(pl.program_id(2) == 0)
    def _():
        xpad_ref[0:pad, :] = jnp.zeros((pad, tc), F32)
        h_ref[...] = jnp.zeros_like(h_ref)

    xpad_ref[pad:pad + ts, :] = xb_ref[...].astype(F32)
    cw = cw_ref[...]
    xc = cb_ref[...] + cw[CONV_WIDTH - 1:CONV_WIDTH, :] * xpad_ref[pad:pad + ts, :]
    for k in range(1, CONV_WIDTH):
        w_row = cw[CONV_WIDTH - 1 - k:CONV_WIDTH - k, :]
        xc = xc + w_row * xpad_ref[pad - k:pad - k + ts, :]
    xpad_ref[0:pad, :] = xpad_ref[ts:ts + pad, :]

    xcb = xc.astype(BF16)
    gates = [_dot(xcb[:, h * LRU_BLOCK:(h + 1) * LRU_BLOCK], wax_ref[h]) for h in range(tc // LRU_BLOCK)]
    ga = jnp.concatenate([g[:, :LRU_BLOCK] for g in gates], axis=1)
    gx = jnp.concatenate([g[:, LRU_BLOCK:] for g in gates], axis=1)
    r = jax.nn.sigmoid(ga + ba_ref[...])
    i = jax.nn.sigmoid(gx + bx_ref[...])
    log_a = (-LRU_C) * r * _softplus(-lam_ref[...])
    a = jnp.exp(log_a)
    b = jnp.sqrt(-jnp.tanh(log_a) * (a * a + 1.0)) * (i * xc)

    row = lax.broadcasted_iota(jnp.int32, (ts, tc), 0)
    b = jnp.where(row == 0, a * h_ref[...] + b, b)
    d = 1
    while d < ts:
        keep = row >= d
        b = b + a * jnp.where(keep, pltpu.roll(b, d, 0), 0.0)
        if 2 * d < ts:
            a = a * jnp.where(keep, pltpu.roll(a, d, 0), 1.0)
        d *= 2
    h_ref[...] = b[ts - 1:ts, :]
    o_ref[...] = (jax.nn.gelu(yb_ref[...].astype(F32), approximate=True) * b).astype(BF16)


def _rglru(zb, conv_w, conv_b, w_ax, b_a, b_x, lam, batch, seq):
    t = zb.shape[0]
    ts, tc = 256, 512
    ns = seq // ts
    ncb = D_MODEL // tc
    chan = lambda b, c, s: (0, c)
    return pl.pallas_call(
        _rglru_kernel,
        grid=(batch, ncb, ns),
        in_specs=[pl.BlockSpec((ts, tc), lambda b, c, s: (b * ns + s, c)),
                  pl.BlockSpec((ts, tc), lambda b, c, s: (b * ns + s, ncb + c)),
                  pl.BlockSpec((CONV_WIDTH, tc), chan),
                  pl.BlockSpec((1, tc), chan),
                  pl.BlockSpec((tc // LRU_BLOCK, LRU_BLOCK, 2 * LRU_BLOCK), lambda b, c, s: (c, 0, 0)),
                  pl.BlockSpec((1, tc), chan),
                  pl.BlockSpec((1, tc), chan),
                  pl.BlockSpec((1, tc), chan)],
        out_specs=pl.BlockSpec((ts, tc), lambda b, c, s: (b * ns + s, c)),
        out_shape=jax.ShapeDtypeStruct((t, D_MODEL), BF16),
        scratch_shapes=[pltpu.VMEM((ts + 8, tc), F32), pltpu.VMEM((1, tc), F32)],
        compiler_params=_params("parallel", "parallel", "arbitrary"),
        name="rglru",
    )(zb, zb, conv_w, conv_b, w_ax, b_a, b_x, lam)


def _out_proj_kernel(attn_ref, rec_ref, ga_ref, gr_ref, x_ref, w_ref, o_ref):
    merged = (jax.nn.sigmoid(ga_ref[...].astype(F32)) * attn_ref[...].astype(F32)
              + jax.nn.sigmoid(gr_ref[...].astype(F32)) * rec_ref[...].astype(F32))
    o_ref[...] = x_ref[...] + _dot(merged.astype(BF16), w_ref[...])


def _out_proj(attn, rec, zb, x2, w_out):
    t = x2.shape[0]
    tm = 512
    row = lambda i: (i, 0)
    return pl.pallas_call(
        _out_proj_kernel,
        grid=(t // tm,),
        in_specs=[pl.BlockSpec((tm, D_MODEL), row),
                  pl.BlockSpec((tm, D_MODEL), row),
                  pl.BlockSpec((tm, D_MODEL), lambda i: (i, 2)),
                  pl.BlockSpec((tm, D_MODEL), lambda i: (i, 3)),
                  pl.BlockSpec((tm, D_MODEL), row),
                  pl.BlockSpec((D_MODEL, D_MODEL), lambda i: (0, 0))],
        out_specs=pl.BlockSpec((tm, D_MODEL), row),
        out_shape=jax.ShapeDtypeStruct((t, D_MODEL), F32),
        compiler_params=_params("parallel"),
        name="out_proj",
    )(attn, rec, zb, zb, x2, w_out)


def _mlp_kernel(x_ref, g_ref, wu_ref, wd_ref, o_ref, u_ref):
    @pl.when(pl.program_id(1) == 0)
    def _():
        x = x_ref[...]
        u_ref[...] = _rms(x, g_ref[...]).astype(BF16)
        o_ref[...] = x

    hid = jnp.square(jnp.maximum(_dot(u_ref[...], wu_ref[...]), 0.0)).astype(BF16)
    o_ref[...] += _dot(hid, wd_ref[...])


def _mlp(x2, g_mlp, w_up, w_down):
    t = x2.shape[0]
    tm, tf = 512, 1024
    return pl.pallas_call(
        _mlp_kernel,
        grid=(t // tm, D_FF // tf),
        in_specs=[pl.BlockSpec((tm, D_MODEL), lambda i, j: (i, 0)),
                  pl.BlockSpec((1, D_MODEL), lambda i, j: (0, 0)),
                  pl.BlockSpec((D_MODEL, tf), lambda i, j: (0, j)),
                  pl.BlockSpec((tf, D_MODEL), lambda i, j: (j, 0))],
        out_specs=pl.BlockSpec((tm, D_MODEL), lambda i, j: (i, 0)),
        out_shape=jax.ShapeDtypeStruct((t, D_MODEL), F32),
        scratch_shapes=[pltpu.VMEM((tm, D_MODEL), BF16)],
        compiler_params=_params("parallel", "arbitrary"),
        name="mlp",
    )(x2, g_mlp, w_up, w_down)


def _ple_kernel(x_ref, g_ref, wg_ref, p_ref, wp_ref, gf_ref, o_ref, *, final_norm):
    x = x_ref[...]
    gate = jax.nn.sigmoid(_dot(_rms(x, g_ref[...]).astype(BF16), wg_ref[...]))
    emb = _dot(p_ref[...].astype(BF16), wp_ref[...])
    y = x + gate * emb
    o_ref[...] = _rms(y, gf_ref[...]) if final_norm else y


def _ple(x2, g_ple, w_gate, p2, w_proj, g_final, final_norm):
    t = x2.shape[0]
    tm = 512
    row = lambda i: (i, 0)
    fix = lambda i: (0, 0)
    return pl.pallas_call(
        functools.partial(_ple_kernel, final_norm=final_norm),
        grid=(t // tm,),
        in_specs=[pl.BlockSpec((tm, D_MODEL), row),
                  pl.BlockSpec((1, D_MODEL), fix),
                  pl.BlockSpec((D_MODEL, D_MODEL), fix),
                  pl.BlockSpec((tm, PLE_DIM), row),
                  pl.BlockSpec((PLE_DIM, D_MODEL), fix),
                  pl.BlockSpec((1, D_MODEL), fix)],
        out_specs=pl.BlockSpec((tm, D_MODEL), row),
        out_shape=jax.ShapeDtypeStruct((t, D_MODEL), F32),
        compiler_params=_params("parallel"),
        name="ple",
    )(x2, g_ple, w_gate, p2, w_proj, g_final)


def _prep_in_proj(w_in):
    o = Q_RANK + KV_RANK
    k1 = w_in[:, o:o + HALF]
    k2 = w_in[:, o + HALF:o + QK_ROPE]
    zpad = jnp.zeros((D_MODEL, LANE - QK_ROPE), w_in.dtype)
    w_a = jnp.concatenate([w_in[:, :o], k1, k2, zpad, k2, k1, zpad], axis=1)
    w_b = w_in[:, o + QK_ROPE:]
    return w_a, w_b


def _prep_q(w_qb):
    w = w_qb.reshape(Q_RANK, N_HEADS, QK_NOPE + QK_ROPE)
    nope = w[:, :, :QK_NOPE]
    q1 = w[:, :, QK_NOPE:QK_NOPE + HALF]
    q2 = w[:, :, QK_NOPE + HALF:]
    full = jnp.concatenate([nope, q1, q2, q2, q1], axis=2)
    return jnp.transpose(full, (1, 2, 0))


def _prep_kv(w_kvb):
    w = w_kvb.reshape(KV_RANK, N_HEADS, QK_NOPE + V_DIM)
    wk = w[:, :, :QK_NOPE].reshape(KV_RANK, N_HEADS * QK_NOPE)
    wv_t = w[:, :, QK_NOPE:].reshape(KV_RANK, N_HEADS * V_DIM).T
    return wk, wv_t


def kernel(x, p, positions, g_mix, w_in, g_q, w_qb, g_kv, w_kvb, conv_w, conv_b, w_a, b_a, w_x, b_x,
           lru_lambda, w_out, g_mlp, w_up, w_down, g_ple, w_ple_gate, w_ple_proj, g_final):
    batch, seq, d = x.shape
    depth = w_in.shape[0]
    t = batch * seq
    assert d == D_MODEL and seq % TOK == 0

    ct_tab, st_tab, c_tab, s_tab = _rope_tables(positions)
    x2 = x.reshape(t, d)
    row = lambda v: v.reshape(1, -1)

    w_in, w_qb, w_kvb, w_out, w_up, w_down, w_ple_gate, w_ple_proj = (
        w.astype(BF16) for w in (w_in, w_qb, w_kvb, w_out, w_up, w_down, w_ple_gate, w_ple_proj))
    w_ax = jnp.concatenate([w_a.astype(BF16), w_x.astype(BF16)], axis=-1)

    for l in range(depth):
        w_in_a, w_in_b = _prep_in_proj(w_in[l])
        wq_t = _prep_q(w_qb[l])
        wk, wv_t = _prep_kv(w_kvb[l])

        cq, ckv, kpe = _mla_down(x2, row(g_mix[l]), w_in_a, row(g_q[l]), row(g_kv[l]), c_tab, s_tab)
        q_t, k, v_t = _mla_up(cq, ckv, ct_tab, st_tab, wq_t, wk, wv_t)
        attn = _attention(q_t, k, kpe, v_t, batch, seq)

        zb = _in_proj(x2, row(g_mix[l]), w_in_b)
        rec = _rglru(zb, conv_w[l], row(conv_b[l]), w_ax[l], row(b_a[l]), row(b_x[l]),
                     row(lru_lambda[l]), batch, seq)

        x2 = _out_proj(attn, rec, zb, x2, w_out[l])
        x2 = _mlp(x2, row(g_mlp[l]), w_up[l], w_down[l])
        x2 = _ple(x2, row(g_ple[l]), w_ple_gate[l], p[l].reshape(t, PLE_DIM), w_ple_proj[l],
                  row(g_final), final_norm=(l == depth - 1))

    return x2.reshape(batch, seq, d)
```

```python
import functools
import math

import jax
import jax.numpy as jnp
from jax import lax
from jax.experimental import pallas as pl
from jax.experimental.pallas import tpu as pltpu

F32 = jnp.float32
BF16 = jnp.bfloat16

D_MODEL = 2048
N_HEADS = 16
QK_NOPE = 128
QK_ROPE = 64
V_DIM = 128
Q_RANK = 512
KV_RANK = 512
ROPE_THETA = 10000.0
LRU_HEADS = 16
LRU_BLOCK = 128
CONV_WIDTH = 4
LRU_C = 8.0
D_FF = 4 * D_MODEL
PLE_DIM = 256
EPS = 1e-6

HALF = QK_ROPE // 2
QK_PAD = 256
LANE = 128
TOK = 512
VMEM_LIMIT = 56 * 1024 * 1024

LOG2E = math.log2(math.e)
Q_SCALE = (QK_NOPE + QK_ROPE) ** -0.5 * LOG2E


def _params(*sem):
    return pltpu.CompilerParams(dimension_semantics=sem, vmem_limit_bytes=VMEM_LIMIT)


def _rms(x, g):
    ms = jnp.mean(x * x, axis=-1, keepdims=True)
    return x * lax.rsqrt(ms + EPS) * g


def _dot(a, b):
    return jnp.dot(a, b, preferred_element_type=F32)


def _dot_nt(a, b):
    return lax.dot_general(a, b, (((1,), (1,)), ((), ())), preferred_element_type=F32)


def _rope_kernel(pos_ref, invf_ref, ct_ref, st_ref, c_ref, s_ref):
    ang = invf_ref[...] * pos_ref[...].astype(F32)
    row = lax.broadcasted_iota(jnp.int32, ang.shape, 0)
    sn = jnp.sin(ang)
    c = jnp.where(row < QK_ROPE, jnp.cos(ang), 0.0)
    s = jnp.where(row < HALF, -sn, jnp.where(row < QK_ROPE, sn, 0.0))
    ct_ref[...] = c
    st_ref[...] = s
    c_ref[...] = c.T
    s_ref[...] = s.T


def _rope_tables(positions):
    t = positions.size
    tb = 2048
    inv_freq = jnp.power(jnp.float32(ROPE_THETA), -jnp.arange(HALF, dtype=F32) * (2.0 / QK_ROPE))
    invf = jnp.tile(inv_freq, LANE // HALF).reshape(LANE, 1)
    pos = positions.reshape(1, t)
    return pl.pallas_call(
        _rope_kernel,
        grid=(t // tb,),
        in_specs=[pl.BlockSpec((1, tb), lambda i: (0, i)),
                  pl.BlockSpec((LANE, 1), lambda i: (0, 0))],
        out_specs=[pl.BlockSpec((LANE, tb), lambda i: (0, i)),
                   pl.BlockSpec((LANE, tb), lambda i: (0, i)),
                   pl.BlockSpec((tb, LANE), lambda i: (i, 0)),
                   pl.BlockSpec((tb, LANE), lambda i: (i, 0))],
        out_shape=[jax.ShapeDtypeStruct((LANE, t), F32), jax.ShapeDtypeStruct((LANE, t), F32),
                   jax.ShapeDtypeStruct((t, LANE), F32), jax.ShapeDtypeStruct((t, LANE), F32)],
        compiler_params=_params("parallel"),
        name="rope_tables",
    )(pos, invf)


def _mla_down_kernel(x_ref, g_ref, w_ref, gq_ref, gkv_ref, c_ref, s_ref, cq_ref, ckv_ref, kpe_ref):
    u = _rms(x_ref[...], g_ref[...]).astype(BF16)
    z = _dot(u, w_ref[...])
    cq_ref[...] = _rms(z[:, :Q_RANK], gq_ref[...]).astype(BF16)
    ckv_ref[...] = _rms(z[:, Q_RANK:Q_RANK + KV_RANK], gkv_ref[...]).astype(BF16)
    o = Q_RANK + KV_RANK
    kpe = z[:, o:o + LANE] * c_ref[...] + z[:, o + LANE:o + 2 * LANE] * s_ref[...]
    kpe_ref[...] = kpe.astype(BF16)


def _mla_down(x2, g_mix, w_a, g_q, g_kv, c_tab, s_tab, l):
    t = x2.shape[0]
    tm = 512
    na = w_a.shape[2]
    row = lambda i: (i, 0)
    fix = lambda i: (0, 0)
    return pl.pallas_call(
        _mla_down_kernel,
        grid=(t // tm,),
        in_specs=[pl.BlockSpec((tm, D_MODEL), row),
                  pl.BlockSpec((1, D_MODEL), fix),
                  pl.BlockSpec((None, D_MODEL, na), lambda i: (l, 0, 0)),
                  pl.BlockSpec((1, Q_RANK), fix),
                  pl.BlockSpec((1, KV_RANK), fix),
                  pl.BlockSpec((tm, LANE), row),
                  pl.BlockSpec((tm, LANE), row)],
        out_specs=[pl.BlockSpec((tm, Q_RANK), row),
                   pl.BlockSpec((tm, KV_RANK), row),
                   pl.BlockSpec((tm, LANE), row)],
        out_shape=[jax.ShapeDtypeStruct((t, Q_RANK), BF16),
                   jax.ShapeDtypeStruct((t, KV_RANK), BF16),
                   jax.ShapeDtypeStruct((t, LANE), BF16)],
        compiler_params=_params("parallel"),
        name="mla_down",
    )(x2, g_mix, w_a, g_q, g_kv, c_tab, s_tab)


def _mla_up_kernel(cq_ref, ckv_ref, ct_ref, st_ref, wq_ref, wk_ref, wv_ref, q_ref, k_ref, v_ref):
    cq = cq_ref[...]
    ckv = ckv_ref[...]
    k_ref[...] = _dot(ckv, wk_ref[...]).astype(BF16)
    v_ref[...] = _dot_nt(wv_ref[...], ckv).astype(BF16)
    ct = ct_ref[...]
    st = st_ref[...]
    zero = jnp.zeros((QK_PAD - QK_NOPE - QK_ROPE, cq.shape[0]), BF16)
    for h in range(N_HEADS):
        qh = _dot_nt(wq_ref[h], cq)
        q_ref[h, 0:QK_NOPE, :] = (qh[0:QK_NOPE] * Q_SCALE).astype(BF16)
        a = qh[QK_NOPE:QK_NOPE + QK_ROPE]
        b = qh[QK_NOPE + QK_ROPE:QK_NOPE + 2 * QK_ROPE]
        q_ref[h, QK_NOPE:QK_NOPE + QK_ROPE, :] = ((a * ct + b * st) * Q_SCALE).astype(BF16)
        q_ref[h, QK_NOPE + QK_ROPE:QK_PAD, :] = zero


def _mla_up(cq, ckv, ct_tab, st_tab, wq_t, wk, wv_t, l):
    t = cq.shape[0]
    tm = TOK
    nb = t // tm
    row = lambda i: (i, 0)
    return pl.pallas_call(
        _mla_up_kernel,
        grid=(nb,),
        in_specs=[pl.BlockSpec((tm, Q_RANK), row),
                  pl.BlockSpec((tm, KV_RANK), row),
                  pl.BlockSpec((QK_ROPE, tm), lambda i: (0, i)),
                  pl.BlockSpec((QK_ROPE, tm), lambda i: (0, i)),
                  pl.BlockSpec((None, N_HEADS, QK_PAD, Q_RANK), lambda i: (l, 0, 0, 0)),
                  pl.BlockSpec((None, KV_RANK, N_HEADS * QK_NOPE), lambda i: (l, 0, 0)),
                  pl.BlockSpec((None, N_HEADS * V_DIM, KV_RANK), lambda i: (l, 0, 0))],
        out_specs=[pl.BlockSpec((None, N_HEADS, QK_PAD, tm), lambda i: (i, 0, 0, 0)),
                   pl.BlockSpec((tm, N_HEADS * QK_NOPE), row),
                   pl.BlockSpec((None, N_HEADS * V_DIM, tm), lambda i: (i, 0, 0))],
        out_shape=[jax.ShapeDtypeStruct((nb, N_HEADS, QK_PAD, tm), BF16),
                   jax.ShapeDtypeStruct((t, N_HEADS * QK_NOPE), BF16),
                   jax.ShapeDtypeStruct((nb, N_HEADS * V_DIM, tm), BF16)],
        compiler_params=_params("parallel"),
        name="mla_up",
    )(cq, ckv, ct_tab, st_tab, wq_t, wk, wv_t)


KV_HALF = TOK // 2
V_AUG = V_DIM + 16


HPS = 2


def _attn_kernel(q_ref, k_ref, kpe_ref, v_ref, o_ref,
                 kcat_ref, vaug_ref, rc_ref, sa_ref, sb_ref, mxa_ref, mxb_ref, m_ref, acc_ref):
    qi = pl.program_id(2)
    nq = v_ref.shape[0]
    heads = range(HPS)

    @pl.when(qi == 0)
    def _():
        ones = jnp.ones((V_AUG - V_DIM, KV_HALF), BF16)
        for g in heads:
            kcat_ref[g, :, 0:QK_NOPE] = k_ref[:, g * QK_NOPE:(g + 1) * QK_NOPE]
            kcat_ref[g, :, QK_NOPE:QK_PAD] = kpe_ref[...]
            for j in range(nq):
                for hf in range(2):
                    vaug_ref[g, 2 * j + hf, 0:V_DIM, :] = v_ref[j, g * V_DIM:(g + 1) * V_DIM,
                                                                hf * KV_HALF:(hf + 1) * KV_HALF]
                    vaug_ref[g, 2 * j + hf, V_DIM:V_AUG, :] = ones
        rc_ref[...] = (lax.broadcasted_iota(jnp.int32, rc_ref.shape, 0)
                       - lax.broadcasted_iota(jnp.int32, rc_ref.shape, 1))

    q_start = qi * TOK

    def scores(half, s_ref, mx_ref, masked):
        start = pl.multiple_of(half * KV_HALF, KV_HALF)
        for g in heads:
            s = _dot(kcat_ref[g, pl.ds(start, KV_HALF), :], q_ref[g])
            if masked:
                s = jnp.where(rc_ref[...] <= q_start - half * KV_HALF, s, -jnp.inf)
            s_ref[g] = s
            mx_ref[g] = jnp.max(s, axis=0, keepdims=True)

    def update(half, s_ref, mx_ref):
        for g in heads:
            m_old = m_ref[g]
            m_new = jnp.maximum(m_old, mx_ref[g])
            alpha = jnp.exp2(m_old - m_new)
            p = jnp.exp2((s_ref[g] - m_new).astype(BF16))
            acc_ref[g] = alpha * acc_ref[g] + _dot(vaug_ref[g, half], p)
            m_ref[g] = m_new

    m_ref[...] = jnp.full(m_ref.shape, -jnp.inf, F32)
    acc_ref[...] = jnp.zeros(acc_ref.shape, F32)
    scores(0, sa_ref, mxa_ref, masked=True)

    def body(j, carry):
        scores(2 * j + 1, sb_ref, mxb_ref, masked=False)
        update(2 * j, sa_ref, mxa_ref)
        scores(2 * j + 2, sa_ref, mxa_ref, masked=True)
        update(2 * j + 1, sb_ref, mxb_ref)
        return carry

    lax.fori_loop(0, qi, body, 0)
    scores(2 * qi + 1, sb_ref, mxb_ref, masked=True)
    update(2 * qi, sa_ref, mxa_ref)
    update(2 * qi + 1, sb_ref, mxb_ref)
    for g in heads:
        inv_l = 1.0 / acc_ref[g, V_DIM:V_DIM + 1, :]
        o_ref[:, g * V_DIM:(g + 1) * V_DIM] = (acc_ref[g, 0:V_DIM, :] * inv_l).T.astype(BF16)


def _attention(q_t, k, kpe, v_t, batch, seq):
    t = k.shape[0]
    nq = seq // TOK
    return pl.pallas_call(
        _attn_kernel,
        grid=(batch, N_HEADS // HPS, nq),
        in_specs=[pl.BlockSpec((None, HPS, QK_PAD, TOK), lambda b, h, i: (b * nq + i, h, 0, 0)),
                  pl.BlockSpec((seq, HPS * QK_NOPE), lambda b, h, i: (b, h)),
                  pl.BlockSpec((seq, LANE), lambda b, h, i: (b, 0)),
                  pl.BlockSpec((nq, HPS * V_DIM, TOK), lambda b, h, i: (b, h, 0))],
        out_specs=pl.BlockSpec((TOK, HPS * V_DIM), lambda b, h, i: (b * nq + i, h)),
        out_shape=jax.ShapeDtypeStruct((t, N_HEADS * V_DIM), BF16),
        scratch_shapes=[pltpu.VMEM((HPS, seq, QK_PAD), BF16),
                        pltpu.VMEM((HPS, 2 * nq, V_AUG, KV_HALF), BF16),
                        pltpu.VMEM((KV_HALF, TOK), jnp.int32),
                        pltpu.VMEM((HPS, KV_HALF, TOK), F32),
                        pltpu.VMEM((HPS, KV_HALF, TOK), F32),
                        pltpu.VMEM((HPS, 1, TOK), F32),
                        pltpu.VMEM((HPS, 1, TOK), F32),
                        pltpu.VMEM((HPS, 1, TOK), F32),
                        pltpu.VMEM((HPS, V_AUG, TOK), F32)],
        compiler_params=_params("parallel", "parallel", "arbitrary"),
        name="mla_attention",
    )(q_t, k, kpe, v_t)


N_WIDE = 4


def _in_proj_kernel(x_ref, g_ref, w_ref, o_ref, u_ref):
    @pl.when(pl.program_id(1) == 0)
    def _():
        u_ref[...] = _rms(x_ref[...], g_ref[...]).astype(BF16)

    o_ref[...] = _dot(u_ref[...], w_ref[...]).astype(BF16)


def _in_proj(x2, g_mix, w_b, l):
    t = x2.shape[0]
    tm, tn = 1024, 1024
    per = D_MODEL // tn
    return pl.pallas_call(
        _in_proj_kernel,
        grid=(t // tm, N_WIDE * per),
        in_specs=[pl.BlockSpec((tm, D_MODEL), lambda i, j: (i, 0)),
                  pl.BlockSpec((1, D_MODEL), lambda i, j: (0, 0)),
                  pl.BlockSpec((None, D_MODEL, tn), lambda i, j: (l, 0, j))],
        out_specs=pl.BlockSpec((None, tm, tn), lambda i, j: (j // per, i, j % per)),
        out_shape=jax.ShapeDtypeStruct((N_WIDE, t, D_MODEL), BF16),
        scratch_shapes=[pltpu.VMEM((tm, D_MODEL), BF16)],
        compiler_params=_params("parallel", "arbitrary"),
        name="in_proj",
    )(x2, g_mix, w_b)


def _softplus(z):
    return jnp.maximum(z, 0.0) + jnp.log1p(jnp.exp(-jnp.abs(z)))


def _sigmoid(z):
    return 0.5 * jnp.tanh(0.5 * z) + 0.5


def _sqrt_nonneg(x):
    return jnp.where(x > 0.0, x * lax.rsqrt(x), 0.0)


def _rglru_gates_kernel(xb_ref, cw_ref, cb_ref, wax_ref, ba_ref, bx_ref, lam_ref, d_ref, b_ref, xpad_ref):
    ts, tc = xb_ref.shape
    pad = 8

    @pl.when(pl.program_id(2) == 0)
    def _():
        xpad_ref[0:pad, :] = jnp.zeros((pad, tc), F32)

    xpad_ref[pad:pad + ts, :] = xb_ref[...].astype(F32)
    cw = 0.5 * cw_ref[...]
    xh = 0.5 * cb_ref[...] + cw[CONV_WIDTH - 1:CONV_WIDTH, :] * xpad_ref[pad:pad + ts, :]
    for k in range(1, CONV_WIDTH):
        w_row = cw[CONV_WIDTH - 1 - k:CONV_WIDTH - k, :]
        xh = xh + w_row * xpad_ref[pad - k:pad - k + ts, :]
    xpad_ref[0:pad, :] = xpad_ref[ts:ts + pad, :]

    xhb = xh.astype(BF16)
    gates = [_dot(xhb[:, h * LRU_BLOCK:(h + 1) * LRU_BLOCK], wax_ref[h]) for h in range(tc // LRU_BLOCK)]
    ga = jnp.concatenate([g[:, :LRU_BLOCK] for g in gates], axis=1)
    gx = jnp.concatenate([g[:, LRU_BLOCK:] for g in gates], axis=1)
    u = jnp.tanh(ga + 0.5 * ba_ref[...]) + 1.0
    i2 = jnp.tanh(gx + 0.5 * bx_ref[...]) + 1.0
    sp = _softplus(-lam_ref[...])
    a1 = jnp.exp2(((-0.5 * LRU_C * LOG2E) * sp) * u) + 1.0
    d = jnp.tanh(((0.25 * LRU_C) * sp) * u) * a1
    d_ref[...] = d.astype(BF16)
    b_ref[...] = (_sqrt_nonneg(d * a1) * (i2 * xh)).astype(BF16)


def _rglru_gates(zb, conv_w, conv_b, w_ax, b_a, b_x, lam, batch, seq, l):
    t = zb.shape[1]
    ts, tc = 256, 512
    ns = seq // ts
    ncb = D_MODEL // tc
    chan = lambda b, c, s: (0, c)
    tile = lambda b, c, s: (b * ns + s, c)
    return pl.pallas_call(
        _rglru_gates_kernel,
        grid=(batch, ncb, ns),
        in_specs=[pl.BlockSpec((None, ts, tc), lambda b, c, s: (0, b * ns + s, c)),
                  pl.BlockSpec((CONV_WIDTH, tc), chan),
                  pl.BlockSpec((1, tc), chan),
                  pl.BlockSpec((None, tc // LRU_BLOCK, LRU_BLOCK, 2 * LRU_BLOCK), lambda b, c, s: (l, c, 0, 0)),
                  pl.BlockSpec((1, tc), chan),
                  pl.BlockSpec((1, tc), chan),
                  pl.BlockSpec((1, tc), chan)],
        out_specs=[pl.BlockSpec((ts, tc), tile), pl.BlockSpec((ts, tc), tile)],
        out_shape=[jax.ShapeDtypeStruct((t, D_MODEL), BF16), jax.ShapeDtypeStruct((t, D_MODEL), BF16)],
        scratch_shapes=[pltpu.VMEM((ts + 8, tc), F32)],
        compiler_params=_params("parallel", "parallel", "arbitrary"),
        name="rglru_gates",
    )(zb, conv_w, conv_b, w_ax, b_a, b_x, lam)


SCAN_GROUPS = D_MODEL // LANE
SCAN_STEPS = 16


def _rglru_scan_kernel(d_ref, b_ref, y_ref, o_ref, h_ref):
    rows = SCAN_STEPS * SCAN_GROUPS

    @pl.when(pl.program_id(1) == 0)
    def _():
        h_ref[...] = jnp.zeros_like(h_ref)

    def chunk(c, h):
        r0 = pl.multiple_of(c * rows, rows)
        a = 1.0 - d_ref[pl.ds(r0, rows), :].astype(F32)
        b = b_ref[pl.ds(r0, rows), :].astype(F32)
        g = jax.nn.gelu(y_ref[pl.ds(r0, rows), :].astype(F32), approximate=True)
        outs = []
        for s in range(SCAN_STEPS):
            sl = slice(s * SCAN_GROUPS, (s + 1) * SCAN_GROUPS)
            h = a[sl] * h + b[sl]
            outs.append(g[sl] * h)
        o_ref[pl.ds(r0, rows), :] = jnp.concatenate(outs, axis=0).astype(BF16)
        return h

    h_ref[...] = lax.fori_loop(0, d_ref.shape[0] // rows, chunk, h_ref[...])


def _rglru_scan(d, b, zb, batch, seq):
    t = d.shape[0]
    tsb = 512
    nb = seq // tsb
    rows = tsb * SCAN_GROUPS
    view = lambda v: v.reshape(-1, LANE)
    tile = lambda bi, s: (bi * nb + s, 0)
    out = pl.pallas_call(
        _rglru_scan_kernel,
        grid=(batch, nb),
        in_specs=[pl.BlockSpec((rows, LANE), tile),
                  pl.BlockSpec((rows, LANE), tile),
                  pl.BlockSpec((rows, LANE), lambda bi, s: (batch * nb + bi * nb + s, 0))],
        out_specs=pl.BlockSpec((rows, LANE), tile),
        out_shape=jax.ShapeDtypeStruct((t * SCAN_GROUPS, LANE), BF16),
        scratch_shapes=[pltpu.VMEM((SCAN_GROUPS, LANE), F32)],
        compiler_params=_params("parallel", "arbitrary"),
        name="rglru_scan",
    )(view(d), view(b), view(zb))
    return out.reshape(t, D_MODEL)


def _out_proj_kernel(attn_ref, rec_ref, ga_ref, gr_ref, x_ref, w_ref, o_ref):
    merged = (_sigmoid(ga_ref[...].astype(F32)) * attn_ref[...].astype(F32)
              + _sigmoid(gr_ref[...].astype(F32)) * rec_ref[...].astype(F32))
    o_ref[...] = x_ref[...] + _dot(merged.astype(BF16), w_ref[...])


def _out_proj(attn, rec, zb, x2, w_out, l):
    t = x2.shape[0]
    tm = 512
    row = lambda i: (i, 0)
    return pl.pallas_call(
        _out_proj_kernel,
        grid=(t // tm,),
        in_specs=[pl.BlockSpec((tm, D_MODEL), row),
                  pl.BlockSpec((tm, D_MODEL), row),
                  pl.BlockSpec((None, tm, D_MODEL), lambda i: (2, i, 0)),
                  pl.BlockSpec((None, tm, D_MODEL), lambda i: (3, i, 0)),
                  pl.BlockSpec((tm, D_MODEL), row),
                  pl.BlockSpec((None, D_MODEL, D_MODEL), lambda i: (l, 0, 0))],
        out_specs=pl.BlockSpec((tm, D_MODEL), row),
        out_shape=jax.ShapeDtypeStruct((t, D_MODEL), F32),
        compiler_params=_params("parallel"),
        name="out_proj",
    )(attn, rec, zb, zb, x2, w_out)


def _mlp_kernel(x_ref, g_ref, wu_ref, wd_ref, o_ref, u_ref):
    @pl.when(pl.program_id(1) == 0)
    def _():
        x = x_ref[...]
        u_ref[...] = _rms(x, g_ref[...]).astype(BF16)
        o_ref[...] = x

    hid = jnp.square(jnp.maximum(_dot(u_ref[...], wu_ref[...]), 0.0)).astype(BF16)
    o_ref[...] += _dot(hid, wd_ref[...])


def _mlp(x2, g_mlp, w_up, w_down, l):
    t = x2.shape[0]
    tm, tf = 1024, 512
    return pl.pallas_call(
        _mlp_kernel,
        grid=(t // tm, D_FF // tf),
        in_specs=[pl.BlockSpec((tm, D_MODEL), lambda i, j: (i, 0)),
                  pl.BlockSpec((1, D_MODEL), lambda i, j: (0, 0)),
                  pl.BlockSpec((None, D_MODEL, tf), lambda i, j: (l, 0, j)),
                  pl.BlockSpec((None, tf, D_MODEL), lambda i, j: (l, j, 0))],
        out_specs=pl.BlockSpec((tm, D_MODEL), lambda i, j: (i, 0)),
        out_shape=jax.ShapeDtypeStruct((t, D_MODEL), F32),
        scratch_shapes=[pltpu.VMEM((tm, D_MODEL), BF16)],
        compiler_params=_params("parallel", "arbitrary"),
        name="mlp",
    )(x2, g_mlp, w_up, w_down)


def _ple_kernel(x_ref, g_ref, wg_ref, p_ref, wp_ref, gf_ref, o_ref, *, final_norm):
    x = x_ref[...]
    gate = _sigmoid(_dot(_rms(x, g_ref[...]).astype(BF16), wg_ref[...]))
    emb = _dot(p_ref[...].astype(BF16), wp_ref[...])
    y = x + gate * emb
    o_ref[...] = _rms(y, gf_ref[...]) if final_norm else y


def _ple(x2, g_ple, w_gate, p3, w_proj, g_final, l, final_norm):
    t = x2.shape[0]
    tm = 512
    row = lambda i: (i, 0)
    fix = lambda i: (0, 0)
    lay = lambda i: (l, 0, 0)
    return pl.pallas_call(
        functools.partial(_ple_kernel, final_norm=final_norm),
        grid=(t // tm,),
        in_specs=[pl.BlockSpec((tm, D_MODEL), row),
                  pl.BlockSpec((1, D_MODEL), fix),
                  pl.BlockSpec((None, D_MODEL, D_MODEL), lay),
                  pl.BlockSpec((None, tm, PLE_DIM), lambda i: (l, i, 0)),
                  pl.BlockSpec((None, PLE_DIM, D_MODEL), lay),
                  pl.BlockSpec((1, D_MODEL), fix)],
        out_specs=pl.BlockSpec((tm, D_MODEL), row),
        out_shape=jax.ShapeDtypeStruct((t, D_MODEL), F32),
        compiler_params=_params("parallel"),
        name="ple",
    )(x2, g_ple, w_gate, p3, w_proj, g_final)


def _prep_in_proj(w_in):
    o = Q_RANK + KV_RANK
    head = w_in[:, :, :o + QK_ROPE].astype(BF16)
    k1 = head[:, :, o:o + HALF]
    k2 = head[:, :, o + HALF:]
    zpad = jnp.zeros(k1.shape[:2] + (LANE - QK_ROPE,), BF16)
    w_a = jnp.concatenate([head[:, :, :o], k1, k2, zpad, k2, k1, zpad], axis=2)
    w_b = w_in[:, :, o + QK_ROPE:].astype(BF16)
    return w_a, w_b


def _prep_q(w_qb):
    w = w_qb.astype(BF16).reshape(-1, Q_RANK, N_HEADS, QK_NOPE + QK_ROPE)
    nope = w[..., :QK_NOPE]
    q1 = w[..., QK_NOPE:QK_NOPE + HALF]
    q2 = w[..., QK_NOPE + HALF:]
    full = jnp.concatenate([nope, q1, q2, q2, q1], axis=3)
    return jnp.transpose(full, (0, 2, 3, 1))


def _prep_kv(w_kvb):
    w = w_kvb.astype(BF16).reshape(-1, KV_RANK, N_HEADS, QK_NOPE + V_DIM)
    wk = w[..., :QK_NOPE].reshape(-1, KV_RANK, N_HEADS * QK_NOPE)
    wv_t = jnp.transpose(w[..., QK_NOPE:].reshape(-1, KV_RANK, N_HEADS * V_DIM), (0, 2, 1))
    return wk, wv_t


def kernel(x, p, positions, g_mix, w_in, g_q, w_qb, g_kv, w_kvb, conv_w, conv_b, w_a, b_a, w_x, b_x,
           lru_lambda, w_out, g_mlp, w_up, w_down, g_ple, w_ple_gate, w_ple_proj, g_final):
    batch, seq, d = x.shape
    depth = w_in.shape[0]
    t = batch * seq
    assert d == D_MODEL and seq % TOK == 0

    ct_tab, st_tab, c_tab, s_tab = _rope_tables(positions)
    x2 = x.reshape(t, d)
    p3 = p.reshape(depth, t, PLE_DIM)
    row = lambda v: v.reshape(1, -1)

    w_in_a, w_in_b = _prep_in_proj(w_in)
    wq_t = _prep_q(w_qb)
    wk, wv_t = _prep_kv(w_kvb)
    w_ax = jnp.concatenate([w_a.astype(BF16), w_x.astype(BF16)], axis=-1)
    w_out, w_up, w_down, w_ple_gate, w_ple_proj = (
        w.astype(BF16) for w in (w_out, w_up, w_down, w_ple_gate, w_ple_proj))

    for l in range(depth):
        cq, ckv, kpe = _mla_down(x2, row(g_mix[l]), w_in_a, row(g_q[l]), row(g_kv[l]), c_tab, s_tab, l)
        q_t, k, v_t = _mla_up(cq, ckv, ct_tab, st_tab, wq_t, wk, wv_t, l)
        attn = _attention(q_t, k, kpe, v_t, batch, seq)

        zb = _in_proj(x2, row(g_mix[l]), w_in_b, l)
        dec, inp = _rglru_gates(zb, conv_w[l], row(conv_b[l]), w_ax, row(b_a[l]), row(b_x[l]),
                                row(lru_lambda[l]), batch, seq, l)
        rec = _rglru_scan(dec, inp, zb, batch, seq)

        x2 = _out_proj(attn, rec, zb, x2, w_out, l)
        x2 = _mlp(x2, row(g_mlp[l]), w_up, w_down, l)
        x2 = _ple(x2, row(g_ple[l]), w_ple_gate, p3, w_ple_proj, row(g_final), l,
                  final_norm=(l == depth - 1))

    return x2.reshape(batch, seq, d)
```

```python
import functools
import math

import jax
import jax.numpy as jnp
from jax import lax
from jax.experimental import pallas as pl
from jax.experimental.pallas import tpu as pltpu

F32 = jnp.float32
BF16 = jnp.bfloat16

D_MODEL = 2048
N_HEADS = 16
QK_NOPE = 128
QK_ROPE = 64
V_DIM = 128
Q_RANK = 512
KV_RANK = 512
ROPE_THETA = 10000.0
LRU_HEADS = 16
LRU_BLOCK = 128
CONV_WIDTH = 4
LRU_C = 8.0
D_FF = 4 * D_MODEL
PLE_DIM = 256
EPS = 1e-6

HALF = QK_ROPE // 2
QK_PAD = 256
LANE = 128
TOK = 512
VMEM_LIMIT = 56 * 1024 * 1024

LOG2E = math.log2(math.e)
Q_SCALE = (QK_NOPE + QK_ROPE) ** -0.5 * LOG2E


def _params(*sem):
    return pltpu.CompilerParams(dimension_semantics=sem, vmem_limit_bytes=VMEM_LIMIT)


def _rms(x, g):
    ms = jnp.mean(x * x, axis=-1, keepdims=True)
    return x * lax.rsqrt(ms + EPS) * g


def _dot(a, b):
    return jnp.dot(a, b, preferred_element_type=F32)


def _dot_nt(a, b):
    return lax.dot_general(a, b, (((1,), (1,)), ((), ())), preferred_element_type=F32)


def _rope_kernel(pos_ref, invf_ref, ct_ref, st_ref, c_ref, s_ref):
    ang = invf_ref[...] * pos_ref[...].astype(F32)
    row = lax.broadcasted_iota(jnp.int32, ang.shape, 0)
    sn = jnp.sin(ang)
    c = jnp.where(row < QK_ROPE, jnp.cos(ang), 0.0)
    s = jnp.where(row < HALF, -sn, jnp.where(row < QK_ROPE, sn, 0.0))
    ct_ref[...] = c
    st_ref[...] = s
    c_ref[...] = c.T
    s_ref[...] = s.T


def _rope_tables(positions):
    t = positions.size
    tb = 2048
    inv_freq = jnp.power(jnp.float32(ROPE_THETA), -jnp.arange(HALF, dtype=F32) * (2.0 / QK_ROPE))
    invf = jnp.tile(inv_freq, LANE // HALF).reshape(LANE, 1)
    pos = positions.reshape(1, t)
    return pl.pallas_call(
        _rope_kernel,
        grid=(t // tb,),
        in_specs=[pl.BlockSpec((1, tb), lambda i: (0, i)),
                  pl.BlockSpec((LANE, 1), lambda i: (0, 0))],
        out_specs=[pl.BlockSpec((LANE, tb), lambda i: (0, i)),
                   pl.BlockSpec((LANE, tb), lambda i: (0, i)),
                   pl.BlockSpec((tb, LANE), lambda i: (i, 0)),
                   pl.BlockSpec((tb, LANE), lambda i: (i, 0))],
        out_shape=[jax.ShapeDtypeStruct((LANE, t), F32), jax.ShapeDtypeStruct((LANE, t), F32),
                   jax.ShapeDtypeStruct((t, LANE), F32), jax.ShapeDtypeStruct((t, LANE), F32)],
        compiler_params=_params("parallel"),
        name="rope_tables",
    )(pos, invf)


def _mla_down_kernel(x_ref, g_ref, w_ref, gq_ref, gkv_ref, c_ref, s_ref, cq_ref, ckv_ref, kpe_ref):
    u = _rms(x_ref[...], g_ref[...]).astype(BF16)
    z = _dot(u, w_ref[...])
    cq_ref[...] = _rms(z[:, :Q_RANK], gq_ref[...]).astype(BF16)
    ckv_ref[...] = _rms(z[:, Q_RANK:Q_RANK + KV_RANK], gkv_ref[...]).astype(BF16)
    o = Q_RANK + KV_RANK
    kpe = z[:, o:o + LANE] * c_ref[...] + z[:, o + LANE:o + 2 * LANE] * s_ref[...]
    kpe_ref[...] = kpe.astype(BF16)


def _mla_down(x2, g_mix, w_a, g_q, g_kv, c_tab, s_tab, l):
    t = x2.shape[0]
    tm = 512
    na = w_a.shape[2]
    row = lambda i: (i, 0)
    fix = lambda i: (0, 0)
    return pl.pallas_call(
        _mla_down_kernel,
        grid=(t // tm,),
        in_specs=[pl.BlockSpec((tm, D_MODEL), row),
                  pl.BlockSpec((1, D_MODEL), fix),
                  pl.BlockSpec((None, D_MODEL, na), lambda i: (l, 0, 0)),
                  pl.BlockSpec((1, Q_RANK), fix),
                  pl.BlockSpec((1, KV_RANK), fix),
                  pl.BlockSpec((tm, LANE), row),
                  pl.BlockSpec((tm, LANE), row)],
        out_specs=[pl.BlockSpec((tm, Q_RANK), row),
                   pl.BlockSpec((tm, KV_RANK), row),
                   pl.BlockSpec((tm, LANE), row)],
        out_shape=[jax.ShapeDtypeStruct((t, Q_RANK), BF16),
                   jax.ShapeDtypeStruct((t, KV_RANK), BF16),
                   jax.ShapeDtypeStruct((t, LANE), BF16)],
        compiler_params=_params("parallel"),
        name="mla_down",
    )(x2, g_mix, w_a, g_q, g_kv, c_tab, s_tab)


def _mla_up_kernel(cq_ref, ckv_ref, ct_ref, st_ref, wq_ref, wk_ref, wv_ref, q_ref, k_ref, v_ref):
    cq = cq_ref[...]
    ckv = ckv_ref[...]
    k_ref[...] = _dot(ckv, wk_ref[...]).astype(BF16)
    v_ref[...] = _dot_nt(wv_ref[...], ckv).astype(BF16)
    ct = ct_ref[...]
    st = st_ref[...]
    zero = jnp.zeros((QK_PAD - QK_NOPE - QK_ROPE, cq.shape[0]), BF16)
    for h in range(N_HEADS):
        qh = _dot_nt(wq_ref[h], cq)
        q_ref[h, 0:QK_NOPE, :] = (qh[0:QK_NOPE] * Q_SCALE).astype(BF16)
        a = qh[QK_NOPE:QK_NOPE + QK_ROPE]
        b = qh[QK_NOPE + QK_ROPE:QK_NOPE + 2 * QK_ROPE]
        q_ref[h, QK_NOPE:QK_NOPE + QK_ROPE, :] = ((a * ct + b * st) * Q_SCALE).astype(BF16)
        q_ref[h, QK_NOPE + QK_ROPE:QK_PAD, :] = zero


def _mla_up(cq, ckv, ct_tab, st_tab, wq_t, wk, wv_t, l):
    t = cq.shape[0]
    tm = TOK
    nb = t // tm
    row = lambda i: (i, 0)
    return pl.pallas_call(
        _mla_up_kernel,
        grid=(nb,),
        in_specs=[pl.BlockSpec((tm, Q_RANK), row),
                  pl.BlockSpec((tm, KV_RANK), row),
                  pl.BlockSpec((QK_ROPE, tm), lambda i: (0, i)),
                  pl.BlockSpec((QK_ROPE, tm), lambda i: (0, i)),
                  pl.BlockSpec((None, N_HEADS, QK_PAD, Q_RANK), lambda i: (l, 0, 0, 0)),
                  pl.BlockSpec((None, KV_RANK, N_HEADS * QK_NOPE), lambda i: (l, 0, 0)),
                  pl.BlockSpec((None, N_HEADS * V_DIM, KV_RANK), lambda i: (l, 0, 0))],
        out_specs=[pl.BlockSpec((None, N_HEADS, QK_PAD, tm), lambda i: (i, 0, 0, 0)),
                   pl.BlockSpec((tm, N_HEADS * QK_NOPE), row),
                   pl.BlockSpec((None, N_HEADS * V_DIM, tm), lambda i: (i, 0, 0))],
        out_shape=[jax.ShapeDtypeStruct((nb, N_HEADS, QK_PAD, tm), BF16),
                   jax.ShapeDtypeStruct((t, N_HEADS * QK_NOPE), BF16),
                   jax.ShapeDtypeStruct((nb, N_HEADS * V_DIM, tm), BF16)],
        compiler_params=_params("parallel"),
        name="mla_up",
    )(cq, ckv, ct_tab, st_tab, wq_t, wk, wv_t)


KV_HALF = TOK // 2
V_AUG = V_DIM + 16


HPS = 2


def _attn_kernel(q_ref, k_ref, kpe_ref, v_ref, o_ref,
                 kcat_ref, vaug_ref, rc_ref, sa_ref, sb_ref, mxa_ref, mxb_ref, m_ref, acc_ref):
    nq = v_ref.shape[0]
    heads = range(HPS)

    ones = jnp.ones((V_AUG - V_DIM, KV_HALF), BF16)
    for g in heads:
        kcat_ref[g, :, 0:QK_NOPE] = k_ref[:, g * QK_NOPE:(g + 1) * QK_NOPE]
        kcat_ref[g, :, QK_NOPE:QK_PAD] = kpe_ref[...]
        for j in range(nq):
            for hf in range(2):
                vaug_ref[g, 2 * j + hf, 0:V_DIM, :] = v_ref[j, g * V_DIM:(g + 1) * V_DIM,
                                                            hf * KV_HALF:(hf + 1) * KV_HALF]
                vaug_ref[g, 2 * j + hf, V_DIM:V_AUG, :] = ones
    rc_ref[...] = (lax.broadcasted_iota(jnp.int32, rc_ref.shape, 0)
                   - lax.broadcasted_iota(jnp.int32, rc_ref.shape, 1))

    def scores(qi, half, s_ref, mx_ref, mask_thr=None):
        par = qi % 2
        start = half * KV_HALF
        if not isinstance(start, int):
            start = pl.multiple_of(start, KV_HALF)
        for g in heads:
            s = _dot(kcat_ref[g, pl.ds(start, KV_HALF), :], q_ref[qi, g])
            if mask_thr is not None:
                s = jnp.where(rc_ref[...] <= mask_thr, s, -jnp.inf)
            s_ref[par, g] = s
            mx_ref[par, g] = jnp.max(s, axis=0, keepdims=True)

    def update(qi, half, s_ref, mx_ref):
        par = qi % 2
        for g in heads:
            m_old = m_ref[par, g]
            m_new = jnp.maximum(m_old, mx_ref[par, g])
            alpha = jnp.exp2(m_old - m_new)
            p = jnp.exp2((s_ref[par, g] - m_new).astype(BF16))
            acc_ref[par, g] = alpha * acc_ref[par, g] + _dot(vaug_ref[g, half], p)
            m_ref[par, g] = m_new

    def start_block(qi):
        par = qi % 2
        m_ref[par] = jnp.full(m_ref.shape[1:], -jnp.inf, F32)
        acc_ref[par] = jnp.zeros(acc_ref.shape[1:], F32)
        scores(qi, 0, sa_ref, mxa_ref, mask_thr=0 if qi == 0 else None)

    def full_block(qi, j, next_is_diagonal):
        scores(qi, 2 * j + 1, sb_ref, mxb_ref)
        update(qi, 2 * j, sa_ref, mxa_ref)
        scores(qi, 2 * j + 2, sa_ref, mxa_ref, mask_thr=0 if next_is_diagonal else None)
        update(qi, 2 * j + 1, sb_ref, mxb_ref)

    start_block(0)
    for qi in range(nq):
        if qi >= 2:
            def body(j, carry, qi=qi):
                full_block(qi, j, next_is_diagonal=False)
                return carry
            lax.fori_loop(0, qi - 1, body, 0)
        if qi >= 1:
            full_block(qi, qi - 1, next_is_diagonal=True)
        scores(qi, 2 * qi + 1, sb_ref, mxb_ref, mask_thr=-KV_HALF)
        if qi + 1 < nq:
            start_block(qi + 1)
        update(qi, 2 * qi, sa_ref, mxa_ref)
        update(qi, 2 * qi + 1, sb_ref, mxb_ref)
        par = qi % 2
        for g in heads:
            inv_l = 1.0 / acc_ref[par, g, V_DIM:V_DIM + 1, :]
            o_ref[qi * TOK:(qi + 1) * TOK, g * V_DIM:(g + 1) * V_DIM] = (
                acc_ref[par, g, 0:V_DIM, :] * inv_l).T.astype(BF16)


def _attention(q_t, k, kpe, v_t, batch, seq):
    t = k.shape[0]
    nq = seq // TOK
    both = lambda *shape: pltpu.VMEM((2, HPS) + shape, F32)
    return pl.pallas_call(
        _attn_kernel,
        grid=(batch, N_HEADS // HPS),
        in_specs=[pl.BlockSpec((nq, HPS, QK_PAD, TOK), lambda b, h: (b, h, 0, 0)),
                  pl.BlockSpec((seq, HPS * QK_NOPE), lambda b, h: (b, h)),
                  pl.BlockSpec((seq, LANE), lambda b, h: (b, 0)),
                  pl.BlockSpec((nq, HPS * V_DIM, TOK), lambda b, h: (b, h, 0))],
        out_specs=pl.BlockSpec((seq, HPS * V_DIM), lambda b, h: (b, h)),
        out_shape=jax.ShapeDtypeStruct((t, N_HEADS * V_DIM), BF16),
        scratch_shapes=[pltpu.VMEM((HPS, seq, QK_PAD), BF16),
                        pltpu.VMEM((HPS, 2 * nq, V_AUG, KV_HALF), BF16),
                        pltpu.VMEM((KV_HALF, TOK), jnp.int32),
                        both(KV_HALF, TOK), both(KV_HALF, TOK),
                        both(1, TOK), both(1, TOK), both(1, TOK),
                        both(V_AUG, TOK)],
        compiler_params=_params("parallel", "parallel"),
        name="mla_attention",
    )(q_t, k, kpe, v_t)


N_WIDE = 4


def _in_proj_kernel(x_ref, g_ref, w_ref, o_ref, u_ref):
    @pl.when(pl.program_id(1) == 0)
    def _():
        u_ref[...] = _rms(x_ref[...], g_ref[...]).astype(BF16)

    o_ref[...] = _dot(u_ref[...], w_ref[...]).astype(BF16)


def _in_proj(x2, g_mix, w_b, l):
    t = x2.shape[0]
    tm, tn = 1024, 1024
    per = D_MODEL // tn
    return pl.pallas_call(
        _in_proj_kernel,
        grid=(t // tm, N_WIDE * per),
        in_specs=[pl.BlockSpec((tm, D_MODEL), lambda i, j: (i, 0)),
                  pl.BlockSpec((1, D_MODEL), lambda i, j: (0, 0)),
                  pl.BlockSpec((None, D_MODEL, tn), lambda i, j: (l, 0, j))],
        out_specs=pl.BlockSpec((None, tm, tn), lambda i, j: (j // per, i, j % per)),
        out_shape=jax.ShapeDtypeStruct((N_WIDE, t, D_MODEL), BF16),
        scratch_shapes=[pltpu.VMEM((tm, D_MODEL), BF16)],
        compiler_params=_params("parallel", "arbitrary"),
        name="in_proj",
    )(x2, g_mix, w_b)


def _softplus(z):
    return jnp.maximum(z, 0.0) + jnp.log1p(jnp.exp(-jnp.abs(z)))


def _sigmoid(z):
    return 0.5 * jnp.tanh(0.5 * z) + 0.5


def _sqrt_nonneg(x):
    return jnp.where(x > 0.0, x * lax.rsqrt(x), 0.0)


SUBLANES = 8


def _scan_rows(a, b, h0):
    ts, tc = a.shape
    groups = ts // SUBLANES
    a3 = a.reshape(groups, SUBLANES, tc)
    b3 = b.reshape(groups, SUBLANES, tc)
    sub = lax.broadcasted_iota(jnp.int32, a3.shape, 1)
    shift = 1
    while shift < SUBLANES:
        keep = sub >= shift
        b3 = b3 + a3 * jnp.where(keep, pltpu.roll(b3, shift, 1), 0.0)
        a3 = a3 * jnp.where(keep, pltpu.roll(a3, shift, 1), 1.0)
        shift *= 2
    carry = h0
    outs = []
    for g in range(groups):
        hg = b3[g] + a3[g] * carry
        outs.append(hg)
        carry = hg[SUBLANES - 1:SUBLANES, :]
    return jnp.concatenate(outs, axis=0), carry


def _rglru_kernel(xb_ref, yb_ref, cw_ref, cb_ref, wax_ref, ba_ref, bx_ref, lam_ref, o_ref,
                  xpad_ref, h_ref):
    ts, tc = xb_ref.shape
    pad = SUBLANES

    @pl.when(pl.program_id(2) == 0)
    def _():
        xpad_ref[0:pad, :] = jnp.zeros((pad, tc), F32)
        h_ref[...] = jnp.zeros_like(h_ref)

    xpad_ref[pad:pad + ts, :] = xb_ref[...].astype(F32)
    cw = 0.5 * cw_ref[...]
    xh = 0.5 * cb_ref[...] + cw[CONV_WIDTH - 1:CONV_WIDTH, :] * xpad_ref[pad:pad + ts, :]
    for k in range(1, CONV_WIDTH):
        w_row = cw[CONV_WIDTH - 1 - k:CONV_WIDTH - k, :]
        xh = xh + w_row * xpad_ref[pad - k:pad - k + ts, :]
    xpad_ref[0:pad, :] = xpad_ref[ts:ts + pad, :]

    xhb = xh.astype(BF16)
    gates = [_dot(xhb[:, h * LRU_BLOCK:(h + 1) * LRU_BLOCK], wax_ref[h]) for h in range(tc // LRU_BLOCK)]
    ga = jnp.concatenate([g[:, :LRU_BLOCK] for g in gates], axis=1)
    gx = jnp.concatenate([g[:, LRU_BLOCK:] for g in gates], axis=1)
    u = jnp.tanh(ga + 0.5 * ba_ref[...]) + 1.0
    i2 = jnp.tanh(gx + 0.5 * bx_ref[...]) + 1.0
    sp = _softplus(-lam_ref[...])
    a = jnp.exp2(((-0.5 * LRU_C * LOG2E) * sp) * u)
    a1 = a + 1.0
    d = jnp.tanh(((0.25 * LRU_C) * sp) * u) * a1
    b = _sqrt_nonneg(d * a1) * (i2 * xh)

    h, h_ref[...] = _scan_rows(a, b, h_ref[...])
    o_ref[...] = (jax.nn.gelu(yb_ref[...].astype(F32), approximate=True) * h).astype(BF16)


def _rglru(zb, conv_w, conv_b, w_ax, b_a, b_x, lam, batch, seq, l):
    t = zb.shape[1]
    ts, tc = 256, 512
    ns = seq // ts
    ncb = D_MODEL // tc
    chan = lambda b, c, s: (0, c)
    return pl.pallas_call(
        _rglru_kernel,
        grid=(batch, ncb, ns),
        in_specs=[pl.BlockSpec((None, ts, tc), lambda b, c, s: (0, b * ns + s, c)),
                  pl.BlockSpec((None, ts, tc), lambda b, c, s: (1, b * ns + s, c)),
                  pl.BlockSpec((CONV_WIDTH, tc), chan),
                  pl.BlockSpec((1, tc), chan),
                  pl.BlockSpec((None, tc // LRU_BLOCK, LRU_BLOCK, 2 * LRU_BLOCK), lambda b, c, s: (l, c, 0, 0)),
                  pl.BlockSpec((1, tc), chan),
                  pl.BlockSpec((1, tc), chan),
                  pl.BlockSpec((1, tc), chan)],
        out_specs=pl.BlockSpec((ts, tc), lambda b, c, s: (b * ns + s, c)),
        out_shape=jax.ShapeDtypeStruct((t, D_MODEL), BF16),
        scratch_shapes=[pltpu.VMEM((ts + SUBLANES, tc), F32), pltpu.VMEM((1, tc), F32)],
        compiler_params=_params("parallel", "parallel", "arbitrary"),
        name="rglru",
    )(zb, zb, conv_w, conv_b, w_ax, b_a, b_x, lam)


def _out_proj_kernel(attn_ref, rec_ref, ga_ref, gr_ref, x_ref, w_ref, o_ref):
    merged = (_sigmoid(ga_ref[...].astype(F32)) * attn_ref[...].astype(F32)
              + _sigmoid(gr_ref[...].astype(F32)) * rec_ref[...].astype(F32))
    o_ref[...] = x_ref[...] + _dot(merged.astype(BF16), w_ref[...])


def _out_proj(attn, rec, zb, x2, w_out, l):
    t = x2.shape[0]
    tm = 512
    row = lambda i: (i, 0)
    return pl.pallas_call(
        _out_proj_kernel,
        grid=(t // tm,),
        in_specs=[pl.BlockSpec((tm, D_MODEL), row),
                  pl.BlockSpec((tm, D_MODEL), row),
                  pl.BlockSpec((None, tm, D_MODEL), lambda i: (2, i, 0)),
                  pl.BlockSpec((None, tm, D_MODEL), lambda i: (3, i, 0)),
                  pl.BlockSpec((tm, D_MODEL), row),
                  pl.BlockSpec((None, D_MODEL, D_MODEL), lambda i: (l, 0, 0))],
        out_specs=pl.BlockSpec((tm, D_MODEL), row),
        out_shape=jax.ShapeDtypeStruct((t, D_MODEL), F32),
        compiler_params=_params("parallel"),
        name="out_proj",
    )(attn, rec, zb, zb, x2, w_out)


def _mlp_kernel(x_ref, g_ref, wu_ref, wd_ref, o_ref, u_ref):
    @pl.when(pl.program_id(1) == 0)
    def _():
        x = x_ref[...]
        u_ref[...] = _rms(x, g_ref[...]).astype(BF16)
        o_ref[...] = x

    hid = jnp.square(jnp.maximum(_dot(u_ref[...], wu_ref[...]), 0.0)).astype(BF16)
    o_ref[...] += _dot(hid, wd_ref[...])


def _mlp(x2, g_mlp, w_up, w_down, l):
    t = x2.shape[0]
    tm, tf = 1024, 512
    return pl.pallas_call(
        _mlp_kernel,
        grid=(t // tm, D_FF // tf),
        in_specs=[pl.BlockSpec((tm, D_MODEL), lambda i, j: (i, 0)),
                  pl.BlockSpec((1, D_MODEL), lambda i, j: (0, 0)),
                  pl.BlockSpec((None, D_MODEL, tf), lambda i, j: (l, 0, j)),
                  pl.BlockSpec((None, tf, D_MODEL), lambda i, j: (l, j, 0))],
        out_specs=pl.BlockSpec((tm, D_MODEL), lambda i, j: (i, 0)),
        out_shape=jax.ShapeDtypeStruct((t, D_MODEL), F32),
        scratch_shapes=[pltpu.VMEM((tm, D_MODEL), BF16)],
        compiler_params=_params("parallel", "arbitrary"),
        name="mlp",
    )(x2, g_mlp, w_up, w_down)


def _ple_kernel(x_ref, g_ref, wg_ref, p_ref, wp_ref, gf_ref, o_ref, *, final_norm):
    x = x_ref[...]
    gate = _sigmoid(_dot(_rms(x, g_ref[...]).astype(BF16), wg_ref[...]))
    emb = _dot(p_ref[...].astype(BF16), wp_ref[...])
    y = x + gate * emb
    o_ref[...] = _rms(y, gf_ref[...]) if final_norm else y


def _ple(x2, g_ple, w_gate, p3, w_proj, g_final, l, final_norm):
    t = x2.shape[0]
    tm = 512
    row = lambda i: (i, 0)
    fix = lambda i: (0, 0)
    lay = lambda i: (l, 0, 0)
    return pl.pallas_call(
        functools.partial(_ple_kernel, final_norm=final_norm),
        grid=(t // tm,),
        in_specs=[pl.BlockSpec((tm, D_MODEL), row),
                  pl.BlockSpec((1, D_MODEL), fix),
                  pl.BlockSpec((None, D_MODEL, D_MODEL), lay),
                  pl.BlockSpec((None, tm, PLE_DIM), lambda i: (l, i, 0)),
                  pl.BlockSpec((None, PLE_DIM, D_MODEL), lay),
                  pl.BlockSpec((1, D_MODEL), fix)],
        out_specs=pl.BlockSpec((tm, D_MODEL), row),
        out_shape=jax.ShapeDtypeStruct((t, D_MODEL), F32),
        compiler_params=_params("parallel"),
        name="ple",
    )(x2, g_ple, w_gate, p3, w_proj, g_final)


def _prep_in_proj(w_in):
    o = Q_RANK + KV_RANK
    head = w_in[:, :, :o + QK_ROPE].astype(BF16)
    k1 = head[:, :, o:o + HALF]
    k2 = head[:, :, o + HALF:]
    zpad = jnp.zeros(k1.shape[:2] + (LANE - QK_ROPE,), BF16)
    w_a = jnp.concatenate([head[:, :, :o], k1, k2, zpad, k2, k1, zpad], axis=2)
    w_b = w_in[:, :, o + QK_ROPE:].astype(BF16)
    return w_a, w_b


def _prep_q(w_qb):
    w = w_qb.astype(BF16).reshape(-1, Q_RANK, N_HEADS, QK_NOPE + QK_ROPE)
    nope = w[..., :QK_NOPE]
    q1 = w[..., QK_NOPE:QK_NOPE + HALF]
    q2 = w[..., QK_NOPE + HALF:]
    full = jnp.concatenate([nope, q1, q2, q2, q1], axis=3)
    return jnp.transpose(full, (0, 2, 3, 1))


def _prep_kv(w_kvb):
    w = w_kvb.astype(BF16).reshape(-1, KV_RANK, N_HEADS, QK_NOPE + V_DIM)
    wk = w[..., :QK_NOPE].reshape(-1, KV_RANK, N_HEADS * QK_NOPE)
    wv_t = jnp.transpose(w[..., QK_NOPE:].reshape(-1, KV_RANK, N_HEADS * V_DIM), (0, 2, 1))
    return wk, wv_t


def kernel(x, p, positions, g_mix, w_in, g_q, w_qb, g_kv, w_kvb, conv_w, conv_b, w_a, b_a, w_x, b_x,
           lru_lambda, w_out, g_mlp, w_up, w_down, g_ple, w_ple_gate, w_ple_proj, g_final):
    batch, seq, d = x.shape
    depth = w_in.shape[0]
    t = batch * seq
    assert d == D_MODEL and seq % TOK == 0

    ct_tab, st_tab, c_tab, s_tab = _rope_tables(positions)
    x2 = x.reshape(t, d)
    p3 = p.reshape(depth, t, PLE_DIM)
    row = lambda v: v.reshape(1, -1)

    w_in_a, w_in_b = _prep_in_proj(w_in)
    wq_t = _prep_q(w_qb)
    wk, wv_t = _prep_kv(w_kvb)
    w_ax = jnp.concatenate([w_a.astype(BF16), w_x.astype(BF16)], axis=-1)
    w_out, w_up, w_down, w_ple_gate, w_ple_proj = (
        w.astype(BF16) for w in (w_out, w_up, w_down, w_ple_gate, w_ple_proj))

    for l in range(depth):
        cq, ckv, kpe = _mla_down(x2, row(g_mix[l]), w_in_a, row(g_q[l]), row(g_kv[l]), c_tab, s_tab, l)
        q_t, k, v_t = _mla_up(cq, ckv, ct_tab, st_tab, wq_t, wk, wv_t, l)
        attn = _attention(q_t, k, kpe, v_t, batch, seq)

        zb = _in_proj(x2, row(g_mix[l]), w_in_b, l)
        rec = _rglru(zb, conv_w[l], row(conv_b[l]), w_ax, row(b_a[l]), row(b_x[l]),
                     row(lru_lambda[l]), batch, seq, l)

        x2 = _out_proj(attn, rec, zb, x2, w_out, l)
        x2 = _mlp(x2, row(g_mlp[l]), w_up, w_down, l)
        x2 = _ple(x2, row(g_ple[l]), w_ple_gate, p3, w_ple_proj, row(g_final), l,
                  final_norm=(l == depth - 1))

    return x2.reshape(batch, seq, d)
```

```python
import functools
import math

import jax
import jax.numpy as jnp
from jax import lax
from jax.experimental import pallas as pl
from jax.experimental.pallas import tpu as pltpu

F32 = jnp.float32
BF16 = jnp.bfloat16

D_MODEL = 2048
N_HEADS = 16
QK_NOPE = 128
QK_ROPE = 64
V_DIM = 128
Q_RANK = 512
KV_RANK = 512
ROPE_THETA = 10000.0
LRU_HEADS = 16
LRU_BLOCK = 128
CONV_WIDTH = 4
LRU_C = 8.0
D_FF = 4 * D_MODEL
PLE_DIM = 256
EPS = 1e-6

HALF = QK_ROPE // 2
QK_PAD = 256
LANE = 128
TOK = 512
VMEM_LIMIT = 56 * 1024 * 1024

LOG2E = math.log2(math.e)
Q_SCALE = (QK_NOPE + QK_ROPE) ** -0.5 * LOG2E


def _params(*sem):
    return pltpu.CompilerParams(dimension_semantics=sem, vmem_limit_bytes=VMEM_LIMIT)


def _rms(x, g):
    ms = jnp.mean(x * x, axis=-1, keepdims=True)
    return x * lax.rsqrt(ms + EPS) * g


def _dot(a, b):
    return jnp.dot(a, b, preferred_element_type=F32)


def _dot_nt(a, b):
    return lax.dot_general(a, b, (((1,), (1,)), ((), ())), preferred_element_type=F32)


def _rope_kernel(pos_ref, invf_ref, ct_ref, st_ref, c_ref, s_ref):
    ang = invf_ref[...] * pos_ref[...].astype(F32)
    row = lax.broadcasted_iota(jnp.int32, ang.shape, 0)
    sn = jnp.sin(ang)
    c = jnp.where(row < QK_ROPE, jnp.cos(ang), 0.0)
    s = jnp.where(row < HALF, -sn, jnp.where(row < QK_ROPE, sn, 0.0))
    ct_ref[...] = c
    st_ref[...] = s
    c_ref[...] = c.T
    s_ref[...] = s.T


def _rope_tables(positions):
    t = positions.size
    tb = 2048
    inv_freq = jnp.power(jnp.float32(ROPE_THETA), -jnp.arange(HALF, dtype=F32) * (2.0 / QK_ROPE))
    invf = jnp.tile(inv_freq, LANE // HALF).reshape(LANE, 1)
    pos = positions.reshape(1, t)
    return pl.pallas_call(
        _rope_kernel,
        grid=(t // tb,),
        in_specs=[pl.BlockSpec((1, tb), lambda i: (0, i)),
                  pl.BlockSpec((LANE, 1), lambda i: (0, 0))],
        out_specs=[pl.BlockSpec((LANE, tb), lambda i: (0, i)),
                   pl.BlockSpec((LANE, tb), lambda i: (0, i)),
                   pl.BlockSpec((tb, LANE), lambda i: (i, 0)),
                   pl.BlockSpec((tb, LANE), lambda i: (i, 0))],
        out_shape=[jax.ShapeDtypeStruct((LANE, t), F32), jax.ShapeDtypeStruct((LANE, t), F32),
                   jax.ShapeDtypeStruct((t, LANE), F32), jax.ShapeDtypeStruct((t, LANE), F32)],
        compiler_params=_params("parallel"),
        name="rope_tables",
    )(pos, invf)


def _mla_down_kernel(x_ref, g_ref, w_ref, gq_ref, gkv_ref, c_ref, s_ref, u_ref, cq_ref, ckv_ref, kpe_ref):
    u = _rms(x_ref[...], g_ref[...]).astype(BF16)
    u_ref[...] = u
    z = _dot(u, w_ref[...])
    cq_ref[...] = _rms(z[:, :Q_RANK], gq_ref[...]).astype(BF16)
    ckv_ref[...] = _rms(z[:, Q_RANK:Q_RANK + KV_RANK], gkv_ref[...]).astype(BF16)
    o = Q_RANK + KV_RANK
    kpe = z[:, o:o + LANE] * c_ref[...] + z[:, o + LANE:o + 2 * LANE] * s_ref[...]
    kpe_ref[...] = kpe.astype(BF16)


def _mla_down(x2, g_mix, w_a, g_q, g_kv, c_tab, s_tab, l):
    t = x2.shape[0]
    tm = 512
    na = w_a.shape[2]
    row = lambda i: (i, 0)
    fix = lambda i: (0, 0)
    return pl.pallas_call(
        _mla_down_kernel,
        grid=(t // tm,),
        in_specs=[pl.BlockSpec((tm, D_MODEL), row),
                  pl.BlockSpec((1, D_MODEL), fix),
                  pl.BlockSpec((None, D_MODEL, na), lambda i: (l, 0, 0)),
                  pl.BlockSpec((1, Q_RANK), fix),
                  pl.BlockSpec((1, KV_RANK), fix),
                  pl.BlockSpec((tm, LANE), row),
                  pl.BlockSpec((tm, LANE), row)],
        out_specs=[pl.BlockSpec((tm, D_MODEL), row),
                   pl.BlockSpec((tm, Q_RANK), row),
                   pl.BlockSpec((tm, KV_RANK), row),
                   pl.BlockSpec((tm, LANE), row)],
        out_shape=[jax.ShapeDtypeStruct((t, D_MODEL), BF16),
                   jax.ShapeDtypeStruct((t, Q_RANK), BF16),
                   jax.ShapeDtypeStruct((t, KV_RANK), BF16),
                   jax.ShapeDtypeStruct((t, LANE), BF16)],
        compiler_params=_params("parallel"),
        name="mla_down",
    )(x2, g_mix, w_a, g_q, g_kv, c_tab, s_tab)


def _mla_up_kernel(cq_ref, ckv_ref, ct_ref, st_ref, wq_ref, wk_ref, wv_ref, q_ref, k_ref, v_ref):
    cq = cq_ref[...]
    ckv = ckv_ref[...]
    k_ref[...] = _dot(ckv, wk_ref[...]).astype(BF16)
    v_ref[...] = _dot_nt(wv_ref[...], ckv).astype(BF16)
    ct = ct_ref[...]
    st = st_ref[...]
    zero = jnp.zeros((QK_PAD - QK_NOPE - QK_ROPE, cq.shape[0]), BF16)
    for h in range(N_HEADS):
        qh = _dot_nt(wq_ref[h], cq)
        q_ref[h, 0:QK_NOPE, :] = (qh[0:QK_NOPE] * Q_SCALE).astype(BF16)
        a = qh[QK_NOPE:QK_NOPE + QK_ROPE]
        b = qh[QK_NOPE + QK_ROPE:QK_NOPE + 2 * QK_ROPE]
        q_ref[h, QK_NOPE:QK_NOPE + QK_ROPE, :] = ((a * ct + b * st) * Q_SCALE).astype(BF16)
        q_ref[h, QK_NOPE + QK_ROPE:QK_PAD, :] = zero


def _mla_up(cq, ckv, ct_tab, st_tab, wq_t, wk, wv_t, l):
    t = cq.shape[0]
    tm = TOK
    nb = t // tm
    row = lambda i: (i, 0)
    return pl.pallas_call(
        _mla_up_kernel,
        grid=(nb,),
        in_specs=[pl.BlockSpec((tm, Q_RANK), row),
                  pl.BlockSpec((tm, KV_RANK), row),
                  pl.BlockSpec((QK_ROPE, tm), lambda i: (0, i)),
                  pl.BlockSpec((QK_ROPE, tm), lambda i: (0, i)),
                  pl.BlockSpec((None, N_HEADS, QK_PAD, Q_RANK), lambda i: (l, 0, 0, 0)),
                  pl.BlockSpec((None, KV_RANK, N_HEADS * QK_NOPE), lambda i: (l, 0, 0)),
                  pl.BlockSpec((None, N_HEADS * V_DIM, KV_RANK), lambda i: (l, 0, 0))],
        out_specs=[pl.BlockSpec((None, N_HEADS, QK_PAD, tm), lambda i: (i, 0, 0, 0)),
                   pl.BlockSpec((tm, N_HEADS * QK_NOPE), row),
                   pl.BlockSpec((None, N_HEADS * V_DIM, tm), lambda i: (i, 0, 0))],
        out_shape=[jax.ShapeDtypeStruct((nb, N_HEADS, QK_PAD, tm), BF16),
                   jax.ShapeDtypeStruct((t, N_HEADS * QK_NOPE), BF16),
                   jax.ShapeDtypeStruct((nb, N_HEADS * V_DIM, tm), BF16)],
        compiler_params=_params("parallel"),
        name="mla_up",
    )(cq, ckv, ct_tab, st_tab, wq_t, wk, wv_t)


KV_HALF = TOK // 2
V_AUG = V_DIM + 16


HPS = 2


def _attn_kernel(q_ref, k_ref, kpe_ref, v_ref, o_ref,
                 kcat_ref, vaug_ref, rc_ref, sa_ref, sb_ref, mxa_ref, mxb_ref, m_ref, acc_ref):
    nq = v_ref.shape[0]
    heads = range(HPS)

    ones = jnp.ones((V_AUG - V_DIM, KV_HALF), BF16)
    for g in heads:
        kcat_ref[g, :, 0:QK_NOPE] = k_ref[:, g * QK_NOPE:(g + 1) * QK_NOPE]
        kcat_ref[g, :, QK_NOPE:QK_PAD] = kpe_ref[...]
        for j in range(nq):
            for hf in range(2):
                vaug_ref[g, 2 * j + hf, 0:V_DIM, :] = v_ref[j, g * V_DIM:(g + 1) * V_DIM,
                                                            hf * KV_HALF:(hf + 1) * KV_HALF]
                vaug_ref[g, 2 * j + hf, V_DIM:V_AUG, :] = ones
    rc_ref[...] = (lax.broadcasted_iota(jnp.int32, rc_ref.shape, 0)
                   - lax.broadcasted_iota(jnp.int32, rc_ref.shape, 1))

    def scores(qi, half, s_ref, mx_ref, mask_thr=None):
        par = qi % 2
        start = half * KV_HALF
        if not isinstance(start, int):
            start = pl.multiple_of(start, KV_HALF)
        for g in heads:
            s = _dot(kcat_ref[g, pl.ds(start, KV_HALF), :], q_ref[qi, g])
            if mask_thr is not None:
                s = jnp.where(rc_ref[...] <= mask_thr, s, -jnp.inf)
            s_ref[par, g] = s
            mx_ref[par, g] = jnp.max(s, axis=0, keepdims=True)

    def update(qi, half, s_ref, mx_ref):
        par = qi % 2
        for g in heads:
            m_old = m_ref[par, g]
            m_new = jnp.maximum(m_old, mx_ref[par, g])
            alpha = jnp.exp2(m_old - m_new)
            p = jnp.exp2((s_ref[par, g] - m_new).astype(BF16))
            acc_ref[par, g] = alpha * acc_ref[par, g] + _dot(vaug_ref[g, half], p)
            m_ref[par, g] = m_new

    def start_block(qi):
        par = qi % 2
        m_ref[par] = jnp.full(m_ref.shape[1:], -jnp.inf, F32)
        acc_ref[par] = jnp.zeros(acc_ref.shape[1:], F32)
        scores(qi, 0, sa_ref, mxa_ref, mask_thr=0 if qi == 0 else None)

    def full_block(qi, j, next_is_diagonal):
        scores(qi, 2 * j + 1, sb_ref, mxb_ref)
        update(qi, 2 * j, sa_ref, mxa_ref)
        scores(qi, 2 * j + 2, sa_ref, mxa_ref, mask_thr=0 if next_is_diagonal else None)
        update(qi, 2 * j + 1, sb_ref, mxb_ref)

    start_block(0)
    for qi in range(nq):
        if qi >= 2:
            def body(j, carry, qi=qi):
                full_block(qi, j, next_is_diagonal=False)
                return carry
            lax.fori_loop(0, qi - 1, body, 0)
        if qi >= 1:
            full_block(qi, qi - 1, next_is_diagonal=True)
        scores(qi, 2 * qi + 1, sb_ref, mxb_ref, mask_thr=-KV_HALF)
        if qi + 1 < nq:
            start_block(qi + 1)
        update(qi, 2 * qi, sa_ref, mxa_ref)
        update(qi, 2 * qi + 1, sb_ref, mxb_ref)
        par = qi % 2
        for g in heads:
            inv_l = 1.0 / acc_ref[par, g, V_DIM:V_DIM + 1, :]
            o_ref[qi * TOK:(qi + 1) * TOK, g * V_DIM:(g + 1) * V_DIM] = (
                acc_ref[par, g, 0:V_DIM, :] * inv_l).T.astype(BF16)


def _attention(q_t, k, kpe, v_t, batch, seq):
    t = k.shape[0]
    nq = seq // TOK
    both = lambda *shape: pltpu.VMEM((2, HPS) + shape, F32)
    return pl.pallas_call(
        _attn_kernel,
        grid=(batch, N_HEADS // HPS),
        in_specs=[pl.BlockSpec((nq, HPS, QK_PAD, TOK), lambda b, h: (b, h, 0, 0)),
                  pl.BlockSpec((seq, HPS * QK_NOPE), lambda b, h: (b, h)),
                  pl.BlockSpec((seq, LANE), lambda b, h: (b, 0)),
                  pl.BlockSpec((nq, HPS * V_DIM, TOK), lambda b, h: (b, h, 0))],
        out_specs=pl.BlockSpec((seq, HPS * V_DIM), lambda b, h: (b, h)),
        out_shape=jax.ShapeDtypeStruct((t, N_HEADS * V_DIM), BF16),
        scratch_shapes=[pltpu.VMEM((HPS, seq, QK_PAD), BF16),
                        pltpu.VMEM((HPS, 2 * nq, V_AUG, KV_HALF), BF16),
                        pltpu.VMEM((KV_HALF, TOK), jnp.int32),
                        both(KV_HALF, TOK), both(KV_HALF, TOK),
                        both(1, TOK), both(1, TOK), both(1, TOK),
                        both(V_AUG, TOK)],
        compiler_params=_params("parallel", "parallel"),
        name="mla_attention",
    )(q_t, k, kpe, v_t)


N_WIDE = 4


def _in_proj_kernel(u_ref, w_ref, o_ref):
    o_ref[...] = _dot(u_ref[...], w_ref[...]).astype(BF16)


def _in_proj(u, w_b, l):
    t = u.shape[0]
    tm, tn = 2048, 1024
    per = D_MODEL // tn
    return pl.pallas_call(
        _in_proj_kernel,
        grid=(t // tm, N_WIDE * per),
        in_specs=[pl.BlockSpec((tm, D_MODEL), lambda i, j: (i, 0)),
                  pl.BlockSpec((None, D_MODEL, tn), lambda i, j: (l, 0, j))],
        out_specs=pl.BlockSpec((None, tm, tn), lambda i, j: (j // per, i, j % per)),
        out_shape=jax.ShapeDtypeStruct((N_WIDE, t, D_MODEL), BF16),
        compiler_params=_params("parallel", "parallel"),
        name="in_proj",
    )(u, w_b)


def _softplus(z):
    return jnp.maximum(z, 0.0) + jnp.log1p(jnp.exp(-jnp.abs(z)))


def _sigmoid(z):
    return 0.5 * jnp.tanh(0.5 * z) + 0.5


def _sqrt_nonneg(x):
    return jnp.where(x > 0.0, x * lax.rsqrt(x), 0.0)


SUBLANES = 8


def _scan_rows(a, b, h0):
    ts, tc = a.shape
    groups = ts // SUBLANES
    a3 = a.reshape(groups, SUBLANES, tc)
    b3 = b.reshape(groups, SUBLANES, tc)
    sub = lax.broadcasted_iota(jnp.int32, a3.shape, 1)
    shift = 1
    while shift < SUBLANES:
        keep = sub >= shift
        b3 = b3 + a3 * jnp.where(keep, pltpu.roll(b3, shift, 1), 0.0)
        a3 = a3 * jnp.where(keep, pltpu.roll(a3, shift, 1), 1.0)
        shift *= 2
    carry = h0
    outs = []
    for g in range(groups):
        hg = b3[g] + a3[g] * carry
        outs.append(hg)
        carry = hg[SUBLANES - 1:SUBLANES, :]
    return jnp.concatenate(outs, axis=0), carry


def _rglru_kernel(xb_ref, yb_ref, cw_ref, cb_ref, wax_ref, ba_ref, bx_ref, lam_ref, o_ref,
                  xpad_ref, h_ref):
    ts, tc = xb_ref.shape
    pad = SUBLANES

    @pl.when(pl.program_id(2) == 0)
    def _():
        xpad_ref[0:pad, :] = jnp.zeros((pad, tc), F32)
        h_ref[...] = jnp.zeros_like(h_ref)

    xpad_ref[pad:pad + ts, :] = xb_ref[...].astype(F32)
    cw = 0.5 * cw_ref[...]
    xh = 0.5 * cb_ref[...] + cw[CONV_WIDTH - 1:CONV_WIDTH, :] * xpad_ref[pad:pad + ts, :]
    for k in range(1, CONV_WIDTH):
        w_row = cw[CONV_WIDTH - 1 - k:CONV_WIDTH - k, :]
        xh = xh + w_row * xpad_ref[pad - k:pad - k + ts, :]
    xpad_ref[0:pad, :] = xpad_ref[ts:ts + pad, :]

    xhb = xh.astype(BF16)
    gates = [_dot(xhb[:, h * LRU_BLOCK:(h + 1) * LRU_BLOCK], wax_ref[h]) for h in range(tc // LRU_BLOCK)]
    ga = jnp.concatenate([g[:, :LRU_BLOCK] for g in gates], axis=1)
    gx = jnp.concatenate([g[:, LRU_BLOCK:] for g in gates], axis=1)
    u = jnp.tanh(ga + 0.5 * ba_ref[...]) + 1.0
    i2 = jnp.tanh(gx + 0.5 * bx_ref[...]) + 1.0
    sp = _softplus(-lam_ref[...])
    a = jnp.exp2(((-0.5 * LRU_C * LOG2E) * sp) * u)
    a1 = a + 1.0
    d = jnp.tanh(((0.25 * LRU_C) * sp) * u) * a1
    b = _sqrt_nonneg(d * a1) * (i2 * xh)

    h, h_ref[...] = _scan_rows(a, b, h_ref[...])
    o_ref[...] = (jax.nn.gelu(yb_ref[...].astype(F32), approximate=True) * h).astype(BF16)


def _rglru(zb, conv_w, conv_b, w_ax, b_a, b_x, lam, batch, seq, l):
    t = zb.shape[1]
    ts, tc = 256, 512
    ns = seq // ts
    ncb = D_MODEL // tc
    chan = lambda b, c, s: (0, c)
    return pl.pallas_call(
        _rglru_kernel,
        grid=(batch, ncb, ns),
        in_specs=[pl.BlockSpec((None, ts, tc), lambda b, c, s: (0, b * ns + s, c)),
                  pl.BlockSpec((None, ts, tc), lambda b, c, s: (1, b * ns + s, c)),
                  pl.BlockSpec((CONV_WIDTH, tc), chan),
                  pl.BlockSpec((1, tc), chan),
                  pl.BlockSpec((None, tc // LRU_BLOCK, LRU_BLOCK, 2 * LRU_BLOCK), lambda b, c, s: (l, c, 0, 0)),
                  pl.BlockSpec((1, tc), chan),
                  pl.BlockSpec((1, tc), chan),
                  pl.BlockSpec((1, tc), chan)],
        out_specs=pl.BlockSpec((ts, tc), lambda b, c, s: (b * ns + s, c)),
        out_shape=jax.ShapeDtypeStruct((t, D_MODEL), BF16),
        scratch_shapes=[pltpu.VMEM((ts + SUBLANES, tc), F32), pltpu.VMEM((1, tc), F32)],
        compiler_params=_params("parallel", "parallel", "arbitrary"),
        name="rglru",
    )(zb, zb, conv_w, conv_b, w_ax, b_a, b_x, lam)


def _out_proj_kernel(attn_ref, rec_ref, ga_ref, gr_ref, x_ref, w_ref, o_ref):
    merged = (_sigmoid(ga_ref[...].astype(F32)) * attn_ref[...].astype(F32)
              + _sigmoid(gr_ref[...].astype(F32)) * rec_ref[...].astype(F32))
    o_ref[...] = x_ref[...] + _dot(merged.astype(BF16), w_ref[...])


def _out_proj(attn, rec, zb, x2, w_out, l):
    t = x2.shape[0]
    tm = 512
    row = lambda i: (i, 0)
    return pl.pallas_call(
        _out_proj_kernel,
        grid=(t // tm,),
        in_specs=[pl.BlockSpec((tm, D_MODEL), row),
                  pl.BlockSpec((tm, D_MODEL), row),
                  pl.BlockSpec((None, tm, D_MODEL), lambda i: (2, i, 0)),
                  pl.BlockSpec((None, tm, D_MODEL), lambda i: (3, i, 0)),
                  pl.BlockSpec((tm, D_MODEL), row),
                  pl.BlockSpec((None, D_MODEL, D_MODEL), lambda i: (l, 0, 0))],
        out_specs=pl.BlockSpec((tm, D_MODEL), row),
        out_shape=jax.ShapeDtypeStruct((t, D_MODEL), F32),
        compiler_params=_params("parallel"),
        name="out_proj",
    )(attn, rec, zb, zb, x2, w_out)


def _mlp_kernel(x_ref, g_ref, wu_ref, wd_ref, o_ref, u_ref):
    @pl.when(pl.program_id(1) == 0)
    def _():
        x = x_ref[...]
        u_ref[...] = _rms(x, g_ref[...]).astype(BF16)
        o_ref[...] = x

    hid = jnp.square(jnp.maximum(_dot(u_ref[...], wu_ref[...]), 0.0)).astype(BF16)
    o_ref[...] += _dot(hid, wd_ref[...])


def _mlp(x2, g_mlp, w_up, w_down, l):
    t = x2.shape[0]
    tm, tf = 1024, 512
    return pl.pallas_call(
        _mlp_kernel,
        grid=(t // tm, D_FF // tf),
        in_specs=[pl.BlockSpec((tm, D_MODEL), lambda i, j: (i, 0)),
                  pl.BlockSpec((1, D_MODEL), lambda i, j: (0, 0)),
                  pl.BlockSpec((None, D_MODEL, tf), lambda i, j: (l, 0, j)),
                  pl.BlockSpec((None, tf, D_MODEL), lambda i, j: (l, j, 0))],
        out_specs=pl.BlockSpec((tm, D_MODEL), lambda i, j: (i, 0)),
        out_shape=jax.ShapeDtypeStruct((t, D_MODEL), F32),
        scratch_shapes=[pltpu.VMEM((tm, D_MODEL), BF16)],
        compiler_params=_params("parallel", "arbitrary"),
        name="mlp",
    )(x2, g_mlp, w_up, w_down)


def _ple_kernel(x_ref, g_ref, wg_ref, p_ref, wp_ref, gf_ref, o_ref, *, final_norm):
    x = x_ref[...]
    gate = _sigmoid(_dot(_rms(x, g_ref[...]).astype(BF16), wg_ref[...]))
    emb = _dot(p_ref[...].astype(BF16), wp_ref[...])
    y = x + gate * emb
    o_ref[...] = _rms(y, gf_ref[...]) if final_norm else y


def _ple(x2, g_ple, w_gate, p3, w_proj, g_final, l, final_norm):
    t = x2.shape[0]
    tm = 512
    row = lambda i: (i, 0)
    fix = lambda i: (0, 0)
    lay = lambda i: (l, 0, 0)
    return pl.pallas_call(
        functools.partial(_ple_kernel, final_norm=final_norm),
        grid=(t // tm,),
        in_specs=[pl.BlockSpec((tm, D_MODEL), row),
                  pl.BlockSpec((1, D_MODEL), fix),
                  pl.BlockSpec((None, D_MODEL, D_MODEL), lay),
                  pl.BlockSpec((None, tm, PLE_DIM), lambda i: (l, i, 0)),
                  pl.BlockSpec((None, PLE_DIM, D_MODEL), lay),
                  pl.BlockSpec((1, D_MODEL), fix)],
        out_specs=pl.BlockSpec((tm, D_MODEL), row),
        out_shape=jax.ShapeDtypeStruct((t, D_MODEL), F32),
        compiler_params=_params("parallel"),
        name="ple",
    )(x2, g_ple, w_gate, p3, w_proj, g_final)


def _prep_in_proj(w_in):
    o = Q_RANK + KV_RANK
    w = w_in.astype(BF16)
    k1 = w[:, :, o:o + HALF]
    k2 = w[:, :, o + HALF:o + QK_ROPE]
    zpad = jnp.zeros(k1.shape[:2] + (LANE - QK_ROPE,), BF16)
    w_a = jnp.concatenate([w[:, :, :o], k1, k2, zpad, k2, k1, zpad], axis=2)
    w_b = w[:, :, o + QK_ROPE:]
    return w_a, w_b


def _prep_q(w_qb):
    w = w_qb.astype(BF16).reshape(-1, Q_RANK, N_HEADS, QK_NOPE + QK_ROPE)
    nope = w[..., :QK_NOPE]
    q1 = w[..., QK_NOPE:QK_NOPE + HALF]
    q2 = w[..., QK_NOPE + HALF:]
    full = jnp.concatenate([nope, q1, q2, q2, q1], axis=3)
    return jnp.transpose(full, (0, 2, 3, 1))


def _prep_kv(w_kvb):
    w = w_kvb.astype(BF16).reshape(-1, KV_RANK, N_HEADS, QK_NOPE + V_DIM)
    wk = w[..., :QK_NOPE].reshape(-1, KV_RANK, N_HEADS * QK_NOPE)
    wv_t = jnp.transpose(w[..., QK_NOPE:].reshape(-1, KV_RANK, N_HEADS * V_DIM), (0, 2, 1))
    return wk, wv_t


def kernel(x, p, positions, g_mix, w_in, g_q, w_qb, g_kv, w_kvb, conv_w, conv_b, w_a, b_a, w_x, b_x,
           lru_lambda, w_out, g_mlp, w_up, w_down, g_ple, w_ple_gate, w_ple_proj, g_final):
    batch, seq, d = x.shape
    depth = w_in.shape[0]
    t = batch * seq
    assert d == D_MODEL and seq % TOK == 0

    ct_tab, st_tab, c_tab, s_tab = _rope_tables(positions)
    x2 = x.reshape(t, d)
    p3 = p.reshape(depth, t, PLE_DIM)
    row = lambda v: v.reshape(1, -1)

    w_in_a, w_in_b = _prep_in_proj(w_in)
    wq_t = _prep_q(w_qb)
    wk, wv_t = _prep_kv(w_kvb)
    w_ax = jnp.concatenate([w_a.astype(BF16), w_x.astype(BF16)], axis=-1)
    w_out, w_up, w_down, w_ple_gate, w_ple_proj = (
        w.astype(BF16) for w in (w_out, w_up, w_down, w_ple_gate, w_ple_proj))

    for l in range(depth):
        u, cq, ckv, kpe = _mla_down(x2, row(g_mix[l]), w_in_a, row(g_q[l]), row(g_kv[l]), c_tab, s_tab, l)
        q_t, k, v_t = _mla_up(cq, ckv, ct_tab, st_tab, wq_t, wk, wv_t, l)
        attn = _attention(q_t, k, kpe, v_t, batch, seq)

        zb = _in_proj(u, w_in_b, l)
        rec = _rglru(zb, conv_w[l], row(conv_b[l]), w_ax, row(b_a[l]), row(b_x[l]),
                     row(lru_lambda[l]), batch, seq, l)

        x2 = _out_proj(attn, rec, zb, x2, w_out, l)
        x2 = _mlp(x2, row(g_mlp[l]), w_up, w_down, l)
        x2 = _ple(x2, row(g_ple[l]), w_ple_gate, p3, w_ple_proj, row(g_final), l,
                  final_norm=(l == depth - 1))

    return x2.reshape(batch, seq, d)
```

```python
import functools
import math

import jax
import jax.numpy as jnp
from jax import lax
from jax.experimental import pallas as pl
from jax.experimental.pallas import tpu as pltpu

F32 = jnp.float32
BF16 = jnp.bfloat16

D_MODEL = 2048
N_HEADS = 16
QK_NOPE = 128
QK_ROPE = 64
V_DIM = 128
Q_RANK = 512
KV_RANK = 512
ROPE_THETA = 10000.0
LRU_HEADS = 16
LRU_BLOCK = 128
CONV_WIDTH = 4
LRU_C = 8.0
D_FF = 4 * D_MODEL
PLE_DIM = 256
EPS = 1e-6

HALF = QK_ROPE // 2
QK_PAD = 256
LANE = 128
TOK = 512
VMEM_LIMIT = 56 * 1024 * 1024

LOG2E = math.log2(math.e)
Q_SCALE = (QK_NOPE + QK_ROPE) ** -0.5 * LOG2E


def _params(*sem):
    return pltpu.CompilerParams(dimension_semantics=sem, vmem_limit_bytes=VMEM_LIMIT)


def _rms(x, g):
    ms = jnp.mean(x * x, axis=-1, keepdims=True)
    return x * lax.rsqrt(ms + EPS) * g


def _dot(a, b):
    return jnp.dot(a, b, preferred_element_type=F32)


def _dot_nt(a, b):
    return lax.dot_general(a, b, (((1,), (1,)), ((), ())), preferred_element_type=F32)


def _rope_kernel(pos_ref, invf_ref, ct_ref, st_ref, c_ref, s_ref):
    ang = invf_ref[...] * pos_ref[...].astype(F32)
    row = lax.broadcasted_iota(jnp.int32, ang.shape, 0)
    sn = jnp.sin(ang)
    c = jnp.where(row < QK_ROPE, jnp.cos(ang), 0.0)
    s = jnp.where(row < HALF, -sn, jnp.where(row < QK_ROPE, sn, 0.0))
    ct_ref[...] = c
    st_ref[...] = s
    c_ref[...] = c.T
    s_ref[...] = s.T


def _rope_tables(positions):
    t = positions.size
    tb = 2048
    inv_freq = jnp.power(jnp.float32(ROPE_THETA), -jnp.arange(HALF, dtype=F32) * (2.0 / QK_ROPE))
    invf = jnp.tile(inv_freq, LANE // HALF).reshape(LANE, 1)
    pos = positions.reshape(1, t)
    return pl.pallas_call(
        _rope_kernel,
        grid=(t // tb,),
        in_specs=[pl.BlockSpec((1, tb), lambda i: (0, i)),
                  pl.BlockSpec((LANE, 1), lambda i: (0, 0))],
        out_specs=[pl.BlockSpec((LANE, tb), lambda i: (0, i)),
                   pl.BlockSpec((LANE, tb), lambda i: (0, i)),
                   pl.BlockSpec((tb, LANE), lambda i: (i, 0)),
                   pl.BlockSpec((tb, LANE), lambda i: (i, 0))],
        out_shape=[jax.ShapeDtypeStruct((LANE, t), F32), jax.ShapeDtypeStruct((LANE, t), F32),
                   jax.ShapeDtypeStruct((t, LANE), F32), jax.ShapeDtypeStruct((t, LANE), F32)],
        compiler_params=_params("parallel"),
        name="rope_tables",
    )(pos, invf)


def _mla_down_kernel(x_ref, g_ref, w_ref, gq_ref, gkv_ref, c_ref, s_ref, u_ref, cq_ref, ckv_ref, kpe_ref):
    u = _rms(x_ref[...], g_ref[...]).astype(BF16)
    u_ref[...] = u
    z = _dot(u, w_ref[...])
    cq_ref[...] = _rms(z[:, :Q_RANK], gq_ref[...]).astype(BF16)
    ckv_ref[...] = _rms(z[:, Q_RANK:Q_RANK + KV_RANK], gkv_ref[...]).astype(BF16)
    o = Q_RANK + KV_RANK
    kpe = z[:, o:o + LANE] * c_ref[...] + z[:, o + LANE:o + 2 * LANE] * s_ref[...]
    kpe_ref[...] = kpe.astype(BF16)


def _mla_down(x2, g_mix, w_a, g_q, g_kv, c_tab, s_tab, l):
    t = x2.shape[0]
    tm = 512
    na = w_a.shape[2]
    row = lambda i: (i, 0)
    fix = lambda i: (0, 0)
    return pl.pallas_call(
        _mla_down_kernel,
        grid=(t // tm,),
        in_specs=[pl.BlockSpec((tm, D_MODEL), row),
                  pl.BlockSpec((1, D_MODEL), fix),
                  pl.BlockSpec((None, D_MODEL, na), lambda i: (l, 0, 0)),
                  pl.BlockSpec((1, Q_RANK), fix),
                  pl.BlockSpec((1, KV_RANK), fix),
                  pl.BlockSpec((tm, LANE), row),
                  pl.BlockSpec((tm, LANE), row)],
        out_specs=[pl.BlockSpec((tm, D_MODEL), row),
                   pl.BlockSpec((tm, Q_RANK), row),
                   pl.BlockSpec((tm, KV_RANK), row),
                   pl.BlockSpec((tm, LANE), row)],
        out_shape=[jax.ShapeDtypeStruct((t, D_MODEL), BF16),
                   jax.ShapeDtypeStruct((t, Q_RANK), BF16),
                   jax.ShapeDtypeStruct((t, KV_RANK), BF16),
                   jax.ShapeDtypeStruct((t, LANE), BF16)],
        compiler_params=_params("parallel"),
        name="mla_down",
    )(x2, g_mix, w_a, g_q, g_kv, c_tab, s_tab)


def _mla_up_kernel(cq_ref, ckv_ref, ct_ref, st_ref, wq_ref, wk_ref, wv_ref, q_ref, k_ref, v_ref):
    cq = cq_ref[...]
    ckv = ckv_ref[...]
    k_ref[...] = _dot(ckv, wk_ref[...]).astype(BF16)
    v_ref[...] = _dot_nt(wv_ref[...], ckv).astype(BF16)
    ct = ct_ref[...]
    st = st_ref[...]
    zero = jnp.zeros((QK_PAD - QK_NOPE - QK_ROPE, cq.shape[0]), BF16)
    for h in range(N_HEADS):
        qh = _dot_nt(wq_ref[h], cq)
        q_ref[h, 0:QK_NOPE, :] = (qh[0:QK_NOPE] * Q_SCALE).astype(BF16)
        a = qh[QK_NOPE:QK_NOPE + QK_ROPE]
        b = qh[QK_NOPE + QK_ROPE:QK_NOPE + 2 * QK_ROPE]
        q_ref[h, QK_NOPE:QK_NOPE + QK_ROPE, :] = ((a * ct + b * st) * Q_SCALE).astype(BF16)
        q_ref[h, QK_NOPE + QK_ROPE:QK_PAD, :] = zero


def _mla_up(cq, ckv, ct_tab, st_tab, wq_t, wk, wv_t, l):
    t = cq.shape[0]
    tm = TOK
    nb = t // tm
    row = lambda i: (i, 0)
    return pl.pallas_call(
        _mla_up_kernel,
        grid=(nb,),
        in_specs=[pl.BlockSpec((tm, Q_RANK), row),
                  pl.BlockSpec((tm, KV_RANK), row),
                  pl.BlockSpec((QK_ROPE, tm), lambda i: (0, i)),
                  pl.BlockSpec((QK_ROPE, tm), lambda i: (0, i)),
                  pl.BlockSpec((None, N_HEADS, QK_PAD, Q_RANK), lambda i: (l, 0, 0, 0)),
                  pl.BlockSpec((None, KV_RANK, N_HEADS * QK_NOPE), lambda i: (l, 0, 0)),
                  pl.BlockSpec((None, N_HEADS * V_DIM, KV_RANK), lambda i: (l, 0, 0))],
        out_specs=[pl.BlockSpec((None, N_HEADS, QK_PAD, tm), lambda i: (i, 0, 0, 0)),
                   pl.BlockSpec((tm, N_HEADS * QK_NOPE), row),
                   pl.BlockSpec((None, N_HEADS * V_DIM, tm), lambda i: (i, 0, 0))],
        out_shape=[jax.ShapeDtypeStruct((nb, N_HEADS, QK_PAD, tm), BF16),
                   jax.ShapeDtypeStruct((t, N_HEADS * QK_NOPE), BF16),
                   jax.ShapeDtypeStruct((nb, N_HEADS * V_DIM, tm), BF16)],
        compiler_params=_params("parallel"),
        name="mla_up",
    )(cq, ckv, ct_tab, st_tab, wq_t, wk, wv_t)


KV_HALF = TOK // 2
V_AUG = V_DIM + 16


HPS = 2


def _attn_kernel(q_ref, k_ref, kpe_ref, v_ref, o_ref,
                 kcat_ref, vaug_ref, rc_ref, sa_ref, sb_ref, mxa_ref, mxb_ref, m_ref, acc_ref):
    nq = v_ref.shape[0]
    heads = range(HPS)

    ones = jnp.ones((V_AUG - V_DIM, KV_HALF), BF16)
    for g in heads:
        kcat_ref[g, :, 0:QK_NOPE] = k_ref[:, g * QK_NOPE:(g + 1) * QK_NOPE]
        kcat_ref[g, :, QK_NOPE:QK_PAD] = kpe_ref[...]
        for j in range(nq):
            for hf in range(2):
                vaug_ref[g, 2 * j + hf, 0:V_DIM, :] = v_ref[j, g * V_DIM:(g + 1) * V_DIM,
                                                            hf * KV_HALF:(hf + 1) * KV_HALF]
                vaug_ref[g, 2 * j + hf, V_DIM:V_AUG, :] = ones
    rc_ref[...] = (lax.broadcasted_iota(jnp.int32, rc_ref.shape, 0)
                   - lax.broadcasted_iota(jnp.int32, rc_ref.shape, 1))

    def scores(qi, half, s_ref, mx_ref, mask_thr=None):
        par = qi % 2
        start = half * KV_HALF
        if not isinstance(start, int):
            start = pl.multiple_of(start, KV_HALF)
        for g in heads:
            s = _dot(kcat_ref[g, pl.ds(start, KV_HALF), :], q_ref[qi, g])
            if mask_thr is not None:
                s = jnp.where(rc_ref[...] <= mask_thr, s, -jnp.inf)
            s_ref[par, g] = s
            mx_ref[par, g] = jnp.max(s, axis=0, keepdims=True)

    def update(qi, half, s_ref, mx_ref):
        par = qi % 2
        for g in heads:
            m_old = m_ref[par, g]
            m_new = jnp.maximum(m_old, mx_ref[par, g])
            alpha = jnp.exp2(m_old - m_new)
            p = jnp.exp2((s_ref[par, g] - m_new).astype(BF16))
            acc_ref[par, g] = alpha * acc_ref[par, g] + _dot(vaug_ref[g, half], p)
            m_ref[par, g] = m_new

    def start_block(qi):
        par = qi % 2
        m_ref[par] = jnp.full(m_ref.shape[1:], -jnp.inf, F32)
        acc_ref[par] = jnp.zeros(acc_ref.shape[1:], F32)
        scores(qi, 0, sa_ref, mxa_ref, mask_thr=0 if qi == 0 else None)

    def full_block(qi, j, next_is_diagonal):
        scores(qi, 2 * j + 1, sb_ref, mxb_ref)
        update(qi, 2 * j, sa_ref, mxa_ref)
        scores(qi, 2 * j + 2, sa_ref, mxa_ref, mask_thr=0 if next_is_diagonal else None)
        update(qi, 2 * j + 1, sb_ref, mxb_ref)

    start_block(0)
    for qi in range(nq):
        if qi >= 2:
            def body(j, carry, qi=qi):
                full_block(qi, j, next_is_diagonal=False)
                return carry
            lax.fori_loop(0, qi - 1, body, 0)
        if qi >= 1:
            full_block(qi, qi - 1, next_is_diagonal=True)
        scores(qi, 2 * qi + 1, sb_ref, mxb_ref, mask_thr=-KV_HALF)
        if qi + 1 < nq:
            start_block(qi + 1)
        update(qi, 2 * qi, sa_ref, mxa_ref)
        update(qi, 2 * qi + 1, sb_ref, mxb_ref)
        par = qi % 2
        for g in heads:
            inv_l = 1.0 / acc_ref[par, g, V_DIM:V_DIM + 1, :]
            o_ref[qi * TOK:(qi + 1) * TOK, g * V_DIM:(g + 1) * V_DIM] = (
                acc_ref[par, g, 0:V_DIM, :] * inv_l).T.astype(BF16)


def _attention(q_t, k, kpe, v_t, batch, seq):
    t = k.shape[0]
    nq = seq // TOK
    both = lambda *shape: pltpu.VMEM((2, HPS) + shape, F32)
    return pl.pallas_call(
        _attn_kernel,
        grid=(batch, N_HEADS // HPS),
        in_specs=[pl.BlockSpec((nq, HPS, QK_PAD, TOK), lambda b, h: (b, h, 0, 0)),
                  pl.BlockSpec((seq, HPS * QK_NOPE), lambda b, h: (b, h)),
                  pl.BlockSpec((seq, LANE), lambda b, h: (b, 0)),
                  pl.BlockSpec((nq, HPS * V_DIM, TOK), lambda b, h: (b, h, 0))],
        out_specs=pl.BlockSpec((seq, HPS * V_DIM), lambda b, h: (b, h)),
        out_shape=jax.ShapeDtypeStruct((t, N_HEADS * V_DIM), BF16),
        scratch_shapes=[pltpu.VMEM((HPS, seq, QK_PAD), BF16),
                        pltpu.VMEM((HPS, 2 * nq, V_AUG, KV_HALF), BF16),
                        pltpu.VMEM((KV_HALF, TOK), jnp.int32),
                        both(KV_HALF, TOK), both(KV_HALF, TOK),
                        both(1, TOK), both(1, TOK), both(1, TOK),
                        both(V_AUG, TOK)],
        compiler_params=_params("parallel", "parallel"),
        name="mla_attention",
    )(q_t, k, kpe, v_t)


N_WIDE = 4


def _in_proj_kernel(u_ref, w_ref, o_ref):
    o_ref[...] = _dot(u_ref[...], w_ref[...]).astype(BF16)


def _in_proj(u, w_b, l):
    t = u.shape[0]
    tm, tn = 2048, 1024
    per = D_MODEL // tn
    return pl.pallas_call(
        _in_proj_kernel,
        grid=(t // tm, N_WIDE * per),
        in_specs=[pl.BlockSpec((tm, D_MODEL), lambda i, j: (i, 0)),
                  pl.BlockSpec((None, D_MODEL, tn), lambda i, j: (l, 0, j))],
        out_specs=pl.BlockSpec((None, tm, tn), lambda i, j: (j // per, i, j % per)),
        out_shape=jax.ShapeDtypeStruct((N_WIDE, t, D_MODEL), BF16),
        compiler_params=_params("parallel", "parallel"),
        name="in_proj",
    )(u, w_b)


def _softplus(z):
    return jnp.maximum(z, 0.0) + jnp.log1p(jnp.exp(-jnp.abs(z)))


def _sigmoid(z):
    return 0.5 * jnp.tanh(0.5 * z) + 0.5


def _sqrt_nonneg(x):
    return jnp.where(x > 0.0, x * lax.rsqrt(x), 0.0)


SUBLANES = 8


def _scan_rows(a, b, h0):
    ts, tc = a.shape
    groups = ts // SUBLANES
    a3 = a.reshape(groups, SUBLANES, tc)
    b3 = b.reshape(groups, SUBLANES, tc)
    sub = lax.broadcasted_iota(jnp.int32, a3.shape, 1)
    shift = 1
    while shift < SUBLANES:
        keep = sub >= shift
        b3 = b3 + a3 * jnp.where(keep, pltpu.roll(b3, shift, 1), 0.0)
        a3 = a3 * jnp.where(keep, pltpu.roll(a3, shift, 1), 1.0)
        shift *= 2
    carry = h0
    outs = []
    for g in range(groups):
        hg = b3[g] + a3[g] * carry
        outs.append(hg)
        carry = hg[SUBLANES - 1:SUBLANES, :]
    return jnp.concatenate(outs, axis=0), carry


def _rglru_kernel(xb_ref, yb_ref, cw_ref, cb_ref, wax_ref, ba_ref, bx_ref, lam_ref, o_ref,
                  xpad_ref, h_ref):
    ts, tc = xb_ref.shape
    pad = SUBLANES

    @pl.when(pl.program_id(2) == 0)
    def _():
        xpad_ref[0:pad, :] = jnp.zeros((pad, tc), F32)
        h_ref[...] = jnp.zeros_like(h_ref)

    xpad_ref[pad:pad + ts, :] = xb_ref[...].astype(F32)
    cw = 0.5 * cw_ref[...]
    xh = 0.5 * cb_ref[...] + cw[CONV_WIDTH - 1:CONV_WIDTH, :] * xpad_ref[pad:pad + ts, :]
    for k in range(1, CONV_WIDTH):
        w_row = cw[CONV_WIDTH - 1 - k:CONV_WIDTH - k, :]
        xh = xh + w_row * xpad_ref[pad - k:pad - k + ts, :]
    xpad_ref[0:pad, :] = xpad_ref[ts:ts + pad, :]

    xhb = xh.astype(BF16)
    gates = [_dot(xhb[:, h * LRU_BLOCK:(h + 1) * LRU_BLOCK], wax_ref[h]) for h in range(tc // LRU_BLOCK)]
    ga = jnp.concatenate([g[:, :LRU_BLOCK] for g in gates], axis=1)
    gx = jnp.concatenate([g[:, LRU_BLOCK:] for g in gates], axis=1)
    u = jnp.tanh(ga + 0.5 * ba_ref[...]) + 1.0
    i2 = jnp.tanh(gx + 0.5 * bx_ref[...]) + 1.0
    sp = _softplus(-lam_ref[...])
    a = jnp.exp2(((-0.5 * LRU_C * LOG2E) * sp) * u)
    a1 = a + 1.0
    d = jnp.tanh(((0.25 * LRU_C) * sp) * u) * a1
    b = _sqrt_nonneg(d * a1) * (i2 * xh)

    h, h_ref[...] = _scan_rows(a, b, h_ref[...])
    o_ref[...] = (jax.nn.gelu(yb_ref[...].astype(F32), approximate=True) * h).astype(BF16)


def _rglru(zb, conv_w, conv_b, w_ax, b_a, b_x, lam, batch, seq, l):
    t = zb.shape[1]
    ts, tc = 256, 512
    ns = seq // ts
    ncb = D_MODEL // tc
    chan = lambda b, c, s: (0, c)
    return pl.pallas_call(
        _rglru_kernel,
        grid=(batch, ncb, ns),
        in_specs=[pl.BlockSpec((None, ts, tc), lambda b, c, s: (0, b * ns + s, c)),
                  pl.BlockSpec((None, ts, tc), lambda b, c, s: (1, b * ns + s, c)),
                  pl.BlockSpec((CONV_WIDTH, tc), chan),
                  pl.BlockSpec((1, tc), chan),
                  pl.BlockSpec((None, tc // LRU_BLOCK, LRU_BLOCK, 2 * LRU_BLOCK), lambda b, c, s: (l, c, 0, 0)),
                  pl.BlockSpec((1, tc), chan),
                  pl.BlockSpec((1, tc), chan),
                  pl.BlockSpec((1, tc), chan)],
        out_specs=pl.BlockSpec((ts, tc), lambda b, c, s: (b * ns + s, c)),
        out_shape=jax.ShapeDtypeStruct((t, D_MODEL), BF16),
        scratch_shapes=[pltpu.VMEM((ts + SUBLANES, tc), F32), pltpu.VMEM((1, tc), F32)],
        compiler_params=_params("parallel", "parallel", "arbitrary"),
        name="rglru",
    )(zb, zb, conv_w, conv_b, w_ax, b_a, b_x, lam)


def _out_proj_kernel(attn_ref, rec_ref, ga_ref, gr_ref, x_ref, w_ref, o_ref):
    merged = (_sigmoid(ga_ref[...].astype(F32)) * attn_ref[...].astype(F32)
              + _sigmoid(gr_ref[...].astype(F32)) * rec_ref[...].astype(F32))
    o_ref[...] = x_ref[...] + _dot(merged.astype(BF16), w_ref[...])


def _out_proj(attn, rec, zb, x2, w_out, l):
    t = x2.shape[0]
    tm = 512
    row = lambda i: (i, 0)
    return pl.pallas_call(
        _out_proj_kernel,
        grid=(t // tm,),
        in_specs=[pl.BlockSpec((tm, D_MODEL), row),
                  pl.BlockSpec((tm, D_MODEL), row),
                  pl.BlockSpec((None, tm, D_MODEL), lambda i: (2, i, 0)),
                  pl.BlockSpec((None, tm, D_MODEL), lambda i: (3, i, 0)),
                  pl.BlockSpec((tm, D_MODEL), row),
                  pl.BlockSpec((None, D_MODEL, D_MODEL), lambda i: (l, 0, 0))],
        out_specs=pl.BlockSpec((tm, D_MODEL), row),
        out_shape=jax.ShapeDtypeStruct((t, D_MODEL), F32),
        compiler_params=_params("parallel"),
        name="out_proj",
    )(attn, rec, zb, zb, x2, w_out)


def _mlp_kernel(x_ref, g_ref, wu_ref, wd_ref, o_ref, u_ref):
    @pl.when(pl.program_id(1) == 0)
    def _():
        x = x_ref[...]
        u_ref[...] = _rms(x, g_ref[...]).astype(BF16)
        o_ref[...] = x

    hid = jnp.square(jnp.maximum(_dot(u_ref[...], wu_ref[...]), 0.0)).astype(BF16)
    o_ref[...] += _dot(hid, wd_ref[...])


def _mlp(x2, g_mlp, w_up, w_down, l):
    t = x2.shape[0]
    tm, tf = 1024, 512
    return pl.pallas_call(
        _mlp_kernel,
        grid=(t // tm, D_FF // tf),
        in_specs=[pl.BlockSpec((tm, D_MODEL), lambda i, j: (i, 0)),
                  pl.BlockSpec((1, D_MODEL), lambda i, j: (0, 0)),
                  pl.BlockSpec((None, D_MODEL, tf), lambda i, j: (l, 0, j)),
                  pl.BlockSpec((None, tf, D_MODEL), lambda i, j: (l, j, 0))],
        out_specs=pl.BlockSpec((tm, D_MODEL), lambda i, j: (i, 0)),
        out_shape=jax.ShapeDtypeStruct((t, D_MODEL), F32),
        scratch_shapes=[pltpu.VMEM((tm, D_MODEL), BF16)],
        compiler_params=_params("parallel", "arbitrary"),
        name="mlp",
    )(x2, g_mlp, w_up, w_down)


def _ple_kernel(x_ref, g_ref, wg_ref, p_ref, wp_ref, gf_ref, o_ref, *, final_norm):
    x = x_ref[...]
    gate = _sigmoid(_dot(_rms(x, g_ref[...]).astype(BF16), wg_ref[...]))
    emb = _dot(p_ref[...].astype(BF16), wp_ref[...])
    y = x + gate * emb
    o_ref[...] = _rms(y, gf_ref[...]) if final_norm else y


def _ple(x2, g_ple, w_gate, p3, w_proj, g_final, l, final_norm):
    t = x2.shape[0]
    tm = 512
    row = lambda i: (i, 0)
    fix = lambda i: (0, 0)
    lay = lambda i: (l, 0, 0)
    return pl.pallas_call(
        functools.partial(_ple_kernel, final_norm=final_norm),
        grid=(t // tm,),
        in_specs=[pl.BlockSpec((tm, D_MODEL), row),
                  pl.BlockSpec((1, D_MODEL), fix),
                  pl.BlockSpec((None, D_MODEL, D_MODEL), lay),
                  pl.BlockSpec((None, tm, PLE_DIM), lambda i: (l, i, 0)),
                  pl.BlockSpec((None, PLE_DIM, D_MODEL), lay),
                  pl.BlockSpec((1, D_MODEL), fix)],
        out_specs=pl.BlockSpec((tm, D_MODEL), row),
        out_shape=jax.ShapeDtypeStruct((t, D_MODEL), F32),
        compiler_params=_params("parallel"),
        name="ple",
    )(x2, g_ple, w_gate, p3, w_proj, g_final)


def _split_w_in_kernel(w_ref, wa_ref, wb_ref):
    w = w_ref[...]
    o = Q_RANK + KV_RANK
    wb_ref[...] = w[:, o + QK_ROPE:].astype(BF16)
    z = w[:, o:o + LANE]
    lane = lax.broadcasted_iota(jnp.int32, z.shape, 1)
    k_a = jnp.where(lane < QK_ROPE, z, 0.0)
    k_b = jnp.where(lane < HALF, pltpu.roll(z, LANE - HALF, 1),
                    jnp.where(lane < QK_ROPE, pltpu.roll(z, HALF, 1), 0.0))
    wa_ref[...] = jnp.concatenate([w[:, :o], k_a, k_b], axis=1).astype(BF16)


def _prep_in_proj(w_in):
    depth, d, d_in = w_in.shape
    o = Q_RANK + KV_RANK
    tk = 256
    n_a, n_b = o + 2 * LANE, d_in - o - QK_ROPE
    return pl.pallas_call(
        _split_w_in_kernel,
        grid=(depth, d // tk),
        in_specs=[pl.BlockSpec((None, tk, d_in), lambda l, i: (l, i, 0))],
        out_specs=[pl.BlockSpec((None, tk, n_a), lambda l, i: (l, i, 0)),
                   pl.BlockSpec((None, tk, n_b), lambda l, i: (l, i, 0))],
        out_shape=[jax.ShapeDtypeStruct((depth, d, n_a), BF16),
                   jax.ShapeDtypeStruct((depth, d, n_b), BF16)],
        compiler_params=_params("parallel", "parallel"),
        name="split_w_in",
    )(w_in)


def _prep_q(w_qb):
    w = w_qb.astype(BF16).reshape(-1, Q_RANK, N_HEADS, QK_NOPE + QK_ROPE)
    nope = w[..., :QK_NOPE]
    q1 = w[..., QK_NOPE:QK_NOPE + HALF]
    q2 = w[..., QK_NOPE + HALF:]
    full = jnp.concatenate([nope, q1, q2, q2, q1], axis=3)
    return jnp.transpose(full, (0, 2, 3, 1))


def _prep_kv(w_kvb):
    w = w_kvb.astype(BF16).reshape(-1, KV_RANK, N_HEADS, QK_NOPE + V_DIM)
    wk = w[..., :QK_NOPE].reshape(-1, KV_RANK, N_HEADS * QK_NOPE)
    wv_t = jnp.transpose(w[..., QK_NOPE:].reshape(-1, KV_RANK, N_HEADS * V_DIM), (0, 2, 1))
    return wk, wv_t


def kernel(x, p, positions, g_mix, w_in, g_q, w_qb, g_kv, w_kvb, conv_w, conv_b, w_a, b_a, w_x, b_x,
           lru_lambda, w_out, g_mlp, w_up, w_down, g_ple, w_ple_gate, w_ple_proj, g_final):
    batch, seq, d = x.shape
    depth = w_in.shape[0]
    t = batch * seq
    assert d == D_MODEL and seq % TOK == 0

    ct_tab, st_tab, c_tab, s_tab = _rope_tables(positions)
    x2 = x.reshape(t, d)
    p3 = p.reshape(depth, t, PLE_DIM)
    row = lambda v: v.reshape(1, -1)

    w_in_a, w_in_b = _prep_in_proj(w_in)
    wq_t = _prep_q(w_qb)
    wk, wv_t = _prep_kv(w_kvb)
    w_ax = jnp.concatenate([w_a.astype(BF16), w_x.astype(BF16)], axis=-1)
    w_out, w_up, w_down, w_ple_gate, w_ple_proj = (
        w.astype(BF16) for w in (w_out, w_up, w_down, w_ple_gate, w_ple_proj))

    for l in range(depth):
        u, cq, ckv, kpe = _mla_down(x2, row(g_mix[l]), w_in_a, row(g_q[l]), row(g_kv[l]), c_tab, s_tab, l)
        q_t, k, v_t = _mla_up(cq, ckv, ct_tab, st_tab, wq_t, wk, wv_t, l)
        attn = _attention(q_t, k, kpe, v_t, batch, seq)

        zb = _in_proj(u, w_in_b, l)
        rec = _rglru(zb, conv_w[l], row(conv_b[l]), w_ax, row(b_a[l]), row(b_x[l]),
                     row(lru_lambda[l]), batch, seq, l)

        x2 = _out_proj(attn, rec, zb, x2, w_out, l)
        x2 = _mlp(x2, row(g_mlp[l]), w_up, w_down, l)
        x2 = _ple(x2, row(g_ple[l]), w_ple_gate, p3, w_ple_proj, row(g_final), l,
                  final_norm=(l == depth - 1))

    return x2.reshape(batch, seq, d)
```

```python
import functools
import math

import jax
import jax.numpy as jnp
from jax import lax
from jax.experimental import pallas as pl
from jax.experimental.pallas import tpu as pltpu

F32 = jnp.float32
BF16 = jnp.bfloat16

D_MODEL = 2048
N_HEADS = 16
QK_NOPE = 128
QK_ROPE = 64
V_DIM = 128
Q_RANK = 512
KV_RANK = 512
ROPE_THETA = 10000.0
LRU_HEADS = 16
LRU_BLOCK = 128
CONV_WIDTH = 4
LRU_C = 8.0
D_FF = 4 * D_MODEL
PLE_DIM = 256
EPS = 1e-6

HALF = QK_ROPE // 2
QK_PAD = 256
LANE = 128
TOK = 512
VMEM_LIMIT = 56 * 1024 * 1024

LOG2E = math.log2(math.e)
Q_SCALE = (QK_NOPE + QK_ROPE) ** -0.5 * LOG2E


def _params(*sem):
    return pltpu.CompilerParams(dimension_semantics=sem, vmem_limit_bytes=VMEM_LIMIT)


def _rms(x, g):
    ms = jnp.mean(x * x, axis=-1, keepdims=True)
    return x * lax.rsqrt(ms + EPS) * g


def _dot(a, b):
    return jnp.dot(a, b, preferred_element_type=F32)


def _dot_nt(a, b):
    return lax.dot_general(a, b, (((1,), (1,)), ((), ())), preferred_element_type=F32)


def _rope_kernel(pos_ref, invf_ref, ct_ref, st_ref, c_ref, s_ref):
    ang = invf_ref[...] * pos_ref[...].astype(F32)
    row = lax.broadcasted_iota(jnp.int32, ang.shape, 0)
    sn = jnp.sin(ang)
    c = jnp.where(row < QK_ROPE, jnp.cos(ang), 0.0)
    s = jnp.where(row < HALF, -sn, jnp.where(row < QK_ROPE, sn, 0.0))
    ct_ref[...] = c
    st_ref[...] = s
    c_ref[...] = c.T
    s_ref[...] = s.T


def _rope_tables(positions):
    t = positions.size
    tb = 2048
    inv_freq = jnp.power(jnp.float32(ROPE_THETA), -jnp.arange(HALF, dtype=F32) * (2.0 / QK_ROPE))
    invf = jnp.tile(inv_freq, LANE // HALF).reshape(LANE, 1)
    pos = positions.reshape(1, t)
    return pl.pallas_call(
        _rope_kernel,
        grid=(t // tb,),
        in_specs=[pl.BlockSpec((1, tb), lambda i: (0, i)),
                  pl.BlockSpec((LANE, 1), lambda i: (0, 0))],
        out_specs=[pl.BlockSpec((LANE, tb), lambda i: (0, i)),
                   pl.BlockSpec((LANE, tb), lambda i: (0, i)),
                   pl.BlockSpec((tb, LANE), lambda i: (i, 0)),
                   pl.BlockSpec((tb, LANE), lambda i: (i, 0))],
        out_shape=[jax.ShapeDtypeStruct((LANE, t), F32), jax.ShapeDtypeStruct((LANE, t), F32),
                   jax.ShapeDtypeStruct((t, LANE), F32), jax.ShapeDtypeStruct((t, LANE), F32)],
        compiler_params=_params("parallel"),
        name="rope_tables",
    )(pos, invf)


def _mla_down_kernel(x_ref, g_ref, w_ref, gq_ref, gkv_ref, c_ref, s_ref, u_ref, cq_ref, ckv_ref, kpe_ref):
    u = _rms(x_ref[...], g_ref[...]).astype(BF16)
    u_ref[...] = u
    z = _dot(u, w_ref[...])
    cq_ref[...] = _rms(z[:, :Q_RANK], gq_ref[...]).astype(BF16)
    ckv_ref[...] = _rms(z[:, Q_RANK:Q_RANK + KV_RANK], gkv_ref[...]).astype(BF16)
    o = Q_RANK + KV_RANK
    kpe = z[:, o:o + LANE] * c_ref[...] + z[:, o + LANE:o + 2 * LANE] * s_ref[...]
    kpe_ref[...] = kpe.astype(BF16)


def _mla_down(x2, g_mix, w_a, g_q, g_kv, c_tab, s_tab, l):
    t = x2.shape[0]
    tm = 512
    na = w_a.shape[2]
    row = lambda i: (i, 0)
    fix = lambda i: (0, 0)
    return pl.pallas_call(
        _mla_down_kernel,
        grid=(t // tm,),
        in_specs=[pl.BlockSpec((tm, D_MODEL), row),
                  pl.BlockSpec((1, D_MODEL), fix),
                  pl.BlockSpec((None, D_MODEL, na), lambda i: (l, 0, 0)),
                  pl.BlockSpec((1, Q_RANK), fix),
                  pl.BlockSpec((1, KV_RANK), fix),
                  pl.BlockSpec((tm, LANE), row),
                  pl.BlockSpec((tm, LANE), row)],
        out_specs=[pl.BlockSpec((tm, D_MODEL), row),
                   pl.BlockSpec((tm, Q_RANK), row),
                   pl.BlockSpec((tm, KV_RANK), row),
                   pl.BlockSpec((tm, LANE), row)],
        out_shape=[jax.ShapeDtypeStruct((t, D_MODEL), BF16),
                   jax.ShapeDtypeStruct((t, Q_RANK), BF16),
                   jax.ShapeDtypeStruct((t, KV_RANK), BF16),
                   jax.ShapeDtypeStruct((t, LANE), BF16)],
        compiler_params=_params("parallel"),
        name="mla_down",
    )(x2, g_mix, w_a, g_q, g_kv, c_tab, s_tab)


def _mla_up_kernel(cq_ref, ckv_ref, ct_ref, st_ref, wq_ref, wk_ref, wv_ref, q_ref, k_ref, v_ref):
    cq = cq_ref[...]
    ckv = ckv_ref[...]
    k_ref[...] = _dot(ckv, wk_ref[...]).astype(BF16)
    v_ref[...] = _dot_nt(wv_ref[...], ckv).astype(BF16)
    ct = ct_ref[...]
    st = st_ref[...]
    zero = jnp.zeros((QK_PAD - QK_NOPE - QK_ROPE, cq.shape[0]), BF16)
    for h in range(N_HEADS):
        qh = _dot_nt(wq_ref[h], cq)
        q_ref[h, 0:QK_NOPE, :] = (qh[0:QK_NOPE] * Q_SCALE).astype(BF16)
        a = qh[QK_NOPE:QK_NOPE + QK_ROPE]
        b = qh[QK_NOPE + QK_ROPE:QK_NOPE + 2 * QK_ROPE]
        q_ref[h, QK_NOPE:QK_NOPE + QK_ROPE, :] = ((a * ct + b * st) * Q_SCALE).astype(BF16)
        q_ref[h, QK_NOPE + QK_ROPE:QK_PAD, :] = zero


def _mla_up(cq, ckv, ct_tab, st_tab, wq_t, wk, wv_t, l):
    t = cq.shape[0]
    tm = TOK
    nb = t // tm
    row = lambda i: (i, 0)
    return pl.pallas_call(
        _mla_up_kernel,
        grid=(nb,),
        in_specs=[pl.BlockSpec((tm, Q_RANK), row),
                  pl.BlockSpec((tm, KV_RANK), row),
                  pl.BlockSpec((QK_ROPE, tm), lambda i: (0, i)),
                  pl.BlockSpec((QK_ROPE, tm), lambda i: (0, i)),
                  pl.BlockSpec((None, N_HEADS, QK_PAD, Q_RANK), lambda i: (l, 0, 0, 0)),
                  pl.BlockSpec((None, KV_RANK, N_HEADS * QK_NOPE), lambda i: (l, 0, 0)),
                  pl.BlockSpec((None, N_HEADS * V_DIM, KV_RANK), lambda i: (l, 0, 0))],
        out_specs=[pl.BlockSpec((None, N_HEADS, QK_PAD, tm), lambda i: (i, 0, 0, 0)),
                   pl.BlockSpec((tm, N_HEADS * QK_NOPE), row),
                   pl.BlockSpec((None, N_HEADS * V_DIM, tm), lambda i: (i, 0, 0))],
        out_shape=[jax.ShapeDtypeStruct((nb, N_HEADS, QK_PAD, tm), BF16),
                   jax.ShapeDtypeStruct((t, N_HEADS * QK_NOPE), BF16),
                   jax.ShapeDtypeStruct((nb, N_HEADS * V_DIM, tm), BF16)],
        compiler_params=_params("parallel"),
        name="mla_up",
    )(cq, ckv, ct_tab, st_tab, wq_t, wk, wv_t)


KV_HALF = TOK // 2
V_AUG = V_DIM + 16


HPS = 2


def _attn_kernel(q_ref, k_ref, kpe_ref, v_ref, o_ref,
                 kcat_ref, vaug_ref, rc_ref, sa_ref, sb_ref, mxa_ref, mxb_ref, m_ref, acc_ref):
    nq = v_ref.shape[0]
    heads = range(HPS)

    ones = jnp.ones((V_AUG - V_DIM, KV_HALF), BF16)
    for g in heads:
        kcat_ref[g, :, 0:QK_NOPE] = k_ref[:, g * QK_NOPE:(g + 1) * QK_NOPE]
        kcat_ref[g, :, QK_NOPE:QK_PAD] = kpe_ref[...]
        for j in range(nq):
            for hf in range(2):
                vaug_ref[g, 2 * j + hf, 0:V_DIM, :] = v_ref[j, g * V_DIM:(g + 1) * V_DIM,
                                                            hf * KV_HALF:(hf + 1) * KV_HALF]
                vaug_ref[g, 2 * j + hf, V_DIM:V_AUG, :] = ones
    rc_ref[...] = (lax.broadcasted_iota(jnp.int32, rc_ref.shape, 0)
                   - lax.broadcasted_iota(jnp.int32, rc_ref.shape, 1))

    def scores(qi, half, s_ref, mx_ref, mask_thr=None):
        par = qi % 2
        start = half * KV_HALF
        if not isinstance(start, int):
            start = pl.multiple_of(start, KV_HALF)
        for g in heads:
            s = _dot(kcat_ref[g, pl.ds(start, KV_HALF), :], q_ref[qi, g])
            if mask_thr is not None:
                s = jnp.where(rc_ref[...] <= mask_thr, s, -jnp.inf)
            s_ref[par, g] = s
            mx_ref[par, g] = jnp.max(s, axis=0, keepdims=True)

    def update(qi, half, s_ref, mx_ref):
        par = qi % 2
        for g in heads:
            m_old = m_ref[par, g]
            m_new = jnp.maximum(m_old, mx_ref[par, g])
            alpha = jnp.exp2(m_old - m_new)
            p = jnp.exp2((s_ref[par, g] - m_new).astype(BF16))
            acc_ref[par, g] = alpha * acc_ref[par, g] + _dot(vaug_ref[g, half], p)
            m_ref[par, g] = m_new

    def start_block(qi):
        par = qi % 2
        m_ref[par] = jnp.full(m_ref.shape[1:], -jnp.inf, F32)
        acc_ref[par] = jnp.zeros(acc_ref.shape[1:], F32)
        scores(qi, 0, sa_ref, mxa_ref, mask_thr=0 if qi == 0 else None)

    def full_block(qi, j, next_is_diagonal):
        scores(qi, 2 * j + 1, sb_ref, mxb_ref)
        update(qi, 2 * j, sa_ref, mxa_ref)
        scores(qi, 2 * j + 2, sa_ref, mxa_ref, mask_thr=0 if next_is_diagonal else None)
        update(qi, 2 * j + 1, sb_ref, mxb_ref)

    start_block(0)
    for qi in range(nq):
        if qi >= 2:
            def body(j, carry, qi=qi):
                full_block(qi, j, next_is_diagonal=False)
                return carry
            lax.fori_loop(0, qi - 1, body, 0)
        if qi >= 1:
            full_block(qi, qi - 1, next_is_diagonal=True)
        scores(qi, 2 * qi + 1, sb_ref, mxb_ref, mask_thr=-KV_HALF)
        if qi + 1 < nq:
            start_block(qi + 1)
        update(qi, 2 * qi, sa_ref, mxa_ref)
        update(qi, 2 * qi + 1, sb_ref, mxb_ref)
        par = qi % 2
        for g in heads:
            inv_l = 1.0 / acc_ref[par, g, V_DIM:V_DIM + 1, :]
            o_ref[qi * TOK:(qi + 1) * TOK, g * V_DIM:(g + 1) * V_DIM] = (
                acc_ref[par, g, 0:V_DIM, :] * inv_l).T.astype(BF16)


def _attention(q_t, k, kpe, v_t, batch, seq):
    t = k.shape[0]
    nq = seq // TOK
    both = lambda *shape: pltpu.VMEM((2, HPS) + shape, F32)
    return pl.pallas_call(
        _attn_kernel,
        grid=(batch, N_HEADS // HPS),
        in_specs=[pl.BlockSpec((nq, HPS, QK_PAD, TOK), lambda b, h: (b, h, 0, 0)),
                  pl.BlockSpec((seq, HPS * QK_NOPE), lambda b, h: (b, h)),
                  pl.BlockSpec((seq, LANE), lambda b, h: (b, 0)),
                  pl.BlockSpec((nq, HPS * V_DIM, TOK), lambda b, h: (b, h, 0))],
        out_specs=pl.BlockSpec((seq, HPS * V_DIM), lambda b, h: (b, h)),
        out_shape=jax.ShapeDtypeStruct((t, N_HEADS * V_DIM), BF16),
        scratch_shapes=[pltpu.VMEM((HPS, seq, QK_PAD), BF16),
                        pltpu.VMEM((HPS, 2 * nq, V_AUG, KV_HALF), BF16),
                        pltpu.VMEM((KV_HALF, TOK), jnp.int32),
                        both(KV_HALF, TOK), both(KV_HALF, TOK),
                        both(1, TOK), both(1, TOK), both(1, TOK),
                        both(V_AUG, TOK)],
        compiler_params=_params("parallel", "parallel"),
        name="mla_attention",
    )(q_t, k, kpe, v_t)


N_WIDE = 4


def _in_proj_kernel(u_ref, w_ref, o_ref):
    o_ref[...] = _dot(u_ref[...], w_ref[...]).astype(BF16)


def _in_proj(u, w_b, l):
    t = u.shape[0]
    tm, tn = 2048, 1024
    per = D_MODEL // tn
    return pl.pallas_call(
        _in_proj_kernel,
        grid=(t // tm, N_WIDE * per),
        in_specs=[pl.BlockSpec((tm, D_MODEL), lambda i, j: (i, 0)),
                  pl.BlockSpec((None, D_MODEL, tn), lambda i, j: (l, 0, j))],
        out_specs=pl.BlockSpec((None, tm, tn), lambda i, j: (j // per, i, j % per)),
        out_shape=jax.ShapeDtypeStruct((N_WIDE, t, D_MODEL), BF16),
        compiler_params=_params("parallel", "parallel"),
        name="in_proj",
    )(u, w_b)


def _softplus(z):
    return jnp.maximum(z, 0.0) + jnp.log1p(jnp.exp(-jnp.abs(z)))


def _sigmoid(z):
    return 0.5 * jnp.tanh(0.5 * z) + 0.5


def _sqrt_nonneg(x):
    return jnp.where(x > 0.0, x * lax.rsqrt(x), 0.0)


SUBLANES = 8


def _scan_rows(a, b, h0):
    ts, tc = a.shape
    groups = ts // SUBLANES
    a3 = a.reshape(groups, SUBLANES, tc)
    b3 = b.reshape(groups, SUBLANES, tc)
    sub = lax.broadcasted_iota(jnp.int32, a3.shape, 1)
    shift = 1
    while shift < SUBLANES:
        keep = sub >= shift
        b3 = b3 + a3 * jnp.where(keep, pltpu.roll(b3, shift, 1), 0.0)
        a3 = a3 * jnp.where(keep, pltpu.roll(a3, shift, 1), 1.0)
        shift *= 2
    carry = h0
    outs = []
    for g in range(groups):
        hg = b3[g] + a3[g] * carry
        outs.append(hg)
        carry = hg[SUBLANES - 1:SUBLANES, :]
    return jnp.concatenate(outs, axis=0), carry


def _rglru_kernel(xb_ref, yb_ref, cw_ref, cb_ref, wax_ref, ba_ref, bx_ref, lam_ref, o_ref,
                  xpad_ref, h_ref):
    ts, tc = xb_ref.shape
    pad = SUBLANES

    @pl.when(pl.program_id(2) == 0)
    def _():
        xpad_ref[0:pad, :] = jnp.zeros((pad, tc), F32)
        h_ref[...] = jnp.zeros_like(h_ref)

    xpad_ref[pad:pad + ts, :] = xb_ref[...].astype(F32)
    cw = 0.5 * cw_ref[...]
    xh = 0.5 * cb_ref[...] + cw[CONV_WIDTH - 1:CONV_WIDTH, :] * xpad_ref[pad:pad + ts, :]
    for k in range(1, CONV_WIDTH):
        w_row = cw[CONV_WIDTH - 1 - k:CONV_WIDTH - k, :]
        xh = xh + w_row * xpad_ref[pad - k:pad - k + ts, :]
    xpad_ref[0:pad, :] = xpad_ref[ts:ts + pad, :]

    xhb = xh.astype(BF16)
    gates = [_dot(xhb[:, h * LRU_BLOCK:(h + 1) * LRU_BLOCK], wax_ref[h]) for h in range(tc // LRU_BLOCK)]
    ga = jnp.concatenate([g[:, :LRU_BLOCK] for g in gates], axis=1)
    gx = jnp.concatenate([g[:, LRU_BLOCK:] for g in gates], axis=1)
    u = jnp.tanh(ga + 0.5 * ba_ref[...]) + 1.0
    i2 = jnp.tanh(gx + 0.5 * bx_ref[...]) + 1.0
    sp = _softplus(-lam_ref[...])
    a = jnp.exp2(((-0.5 * LRU_C * LOG2E) * sp) * u)
    a1 = a + 1.0
    d = jnp.tanh(((0.25 * LRU_C) * sp) * u) * a1
    b = _sqrt_nonneg(d * a1) * (i2 * xh)

    h, h_ref[...] = _scan_rows(a, b, h_ref[...])
    o_ref[...] = (jax.nn.gelu(yb_ref[...].astype(F32), approximate=True) * h).astype(BF16)


def _rglru(zb, conv_w, conv_b, w_ax, b_a, b_x, lam, batch, seq, l):
    t = zb.shape[1]
    ts, tc = 256, 512
    ns = seq // ts
    ncb = D_MODEL // tc
    chan = lambda b, c, s: (0, c)
    return pl.pallas_call(
        _rglru_kernel,
        grid=(batch, ncb, ns),
        in_specs=[pl.BlockSpec((None, ts, tc), lambda b, c, s: (0, b * ns + s, c)),
                  pl.BlockSpec((None, ts, tc), lambda b, c, s: (1, b * ns + s, c)),
                  pl.BlockSpec((CONV_WIDTH, tc), chan),
                  pl.BlockSpec((1, tc), chan),
                  pl.BlockSpec((None, tc // LRU_BLOCK, LRU_BLOCK, 2 * LRU_BLOCK), lambda b, c, s: (l, c, 0, 0)),
                  pl.BlockSpec((1, tc), chan),
                  pl.BlockSpec((1, tc), chan),
                  pl.BlockSpec((1, tc), chan)],
        out_specs=pl.BlockSpec((ts, tc), lambda b, c, s: (b * ns + s, c)),
        out_shape=jax.ShapeDtypeStruct((t, D_MODEL), BF16),
        scratch_shapes=[pltpu.VMEM((ts + SUBLANES, tc), F32), pltpu.VMEM((1, tc), F32)],
        compiler_params=_params("parallel", "parallel", "arbitrary"),
        name="rglru",
    )(zb, zb, conv_w, conv_b, w_ax, b_a, b_x, lam)


def _out_proj_kernel(attn_ref, rec_ref, ga_ref, gr_ref, x_ref, w_ref, o_ref):
    merged = _sigmoid(ga_ref[...]) * attn_ref[...] + _sigmoid(gr_ref[...]) * rec_ref[...]
    o_ref[...] = x_ref[...] + _dot(merged, w_ref[...])


def _out_proj(attn, rec, zb, x2, w_out, l):
    t = x2.shape[0]
    tm = 512
    row = lambda i: (i, 0)
    return pl.pallas_call(
        _out_proj_kernel,
        grid=(t // tm,),
        in_specs=[pl.BlockSpec((tm, D_MODEL), row),
                  pl.BlockSpec((tm, D_MODEL), row),
                  pl.BlockSpec((None, tm, D_MODEL), lambda i: (2, i, 0)),
                  pl.BlockSpec((None, tm, D_MODEL), lambda i: (3, i, 0)),
                  pl.BlockSpec((tm, D_MODEL), row),
                  pl.BlockSpec((None, D_MODEL, D_MODEL), lambda i: (l, 0, 0))],
        out_specs=pl.BlockSpec((tm, D_MODEL), row),
        out_shape=jax.ShapeDtypeStruct((t, D_MODEL), F32),
        compiler_params=_params("parallel"),
        name="out_proj",
    )(attn, rec, zb, zb, x2, w_out)


def _mlp_kernel(x_ref, g_ref, wu_ref, wd_ref, o_ref, u_ref):
    @pl.when(pl.program_id(1) == 0)
    def _():
        x = x_ref[...]
        u_ref[...] = _rms(x, g_ref[...]).astype(BF16)
        o_ref[...] = x

    hid = jnp.square(jnp.maximum(_dot(u_ref[...], wu_ref[...]), 0.0)).astype(BF16)
    o_ref[...] += _dot(hid, wd_ref[...])


def _mlp(x2, g_mlp, w_up, w_down, l):
    t = x2.shape[0]
    tm, tf = 1024, 512
    return pl.pallas_call(
        _mlp_kernel,
        grid=(t // tm, D_FF // tf),
        in_specs=[pl.BlockSpec((tm, D_MODEL), lambda i, j: (i, 0)),
                  pl.BlockSpec((1, D_MODEL), lambda i, j: (0, 0)),
                  pl.BlockSpec((None, D_MODEL, tf), lambda i, j: (l, 0, j)),
                  pl.BlockSpec((None, tf, D_MODEL), lambda i, j: (l, j, 0))],
        out_specs=pl.BlockSpec((tm, D_MODEL), lambda i, j: (i, 0)),
        out_shape=jax.ShapeDtypeStruct((t, D_MODEL), F32),
        scratch_shapes=[pltpu.VMEM((tm, D_MODEL), BF16)],
        compiler_params=_params("parallel", "arbitrary"),
        name="mlp",
    )(x2, g_mlp, w_up, w_down)


def _ple_kernel(x_ref, g_ref, wg_ref, p_ref, wp_ref, gf_ref, o_ref, *, final_norm):
    x = x_ref[...]
    gate = _sigmoid(_dot(_rms(x, g_ref[...]).astype(BF16), wg_ref[...]))
    emb = _dot(p_ref[...].astype(BF16), wp_ref[...])
    y = x + gate * emb
    o_ref[...] = _rms(y, gf_ref[...]) if final_norm else y


def _ple(x2, g_ple, w_gate, p3, w_proj, g_final, l, final_norm):
    t = x2.shape[0]
    tm = 512
    row = lambda i: (i, 0)
    fix = lambda i: (0, 0)
    lay = lambda i: (l, 0, 0)
    return pl.pallas_call(
        functools.partial(_ple_kernel, final_norm=final_norm),
        grid=(t // tm,),
        in_specs=[pl.BlockSpec((tm, D_MODEL), row),
                  pl.BlockSpec((1, D_MODEL), fix),
                  pl.BlockSpec((None, D_MODEL, D_MODEL), lay),
                  pl.BlockSpec((None, tm, PLE_DIM), lambda i: (l, i, 0)),
                  pl.BlockSpec((None, PLE_DIM, D_MODEL), lay),
                  pl.BlockSpec((1, D_MODEL), fix)],
        out_specs=pl.BlockSpec((tm, D_MODEL), row),
        out_shape=jax.ShapeDtypeStruct((t, D_MODEL), F32),
        compiler_params=_params("parallel"),
        name="ple",
    )(x2, g_ple, w_gate, p3, w_proj, g_final)


def _split_w_in_kernel(w_ref, wa_ref, wb_ref):
    w = w_ref[...]
    o = Q_RANK + KV_RANK
    wb_ref[...] = w[:, o + QK_ROPE:].astype(BF16)
    z = w[:, o:o + LANE]
    lane = lax.broadcasted_iota(jnp.int32, z.shape, 1)
    k_a = jnp.where(lane < QK_ROPE, z, 0.0)
    k_b = jnp.where(lane < HALF, pltpu.roll(z, LANE - HALF, 1),
                    jnp.where(lane < QK_ROPE, pltpu.roll(z, HALF, 1), 0.0))
    wa_ref[...] = jnp.concatenate([w[:, :o], k_a, k_b], axis=1).astype(BF16)


def _prep_in_proj(w_in):
    depth, d, d_in = w_in.shape
    o = Q_RANK + KV_RANK
    tk = 256
    n_a, n_b = o + 2 * LANE, d_in - o - QK_ROPE
    return pl.pallas_call(
        _split_w_in_kernel,
        grid=(depth, d // tk),
        in_specs=[pl.BlockSpec((None, tk, d_in), lambda l, i: (l, i, 0))],
        out_specs=[pl.BlockSpec((None, tk, n_a), lambda l, i: (l, i, 0)),
                   pl.BlockSpec((None, tk, n_b), lambda l, i: (l, i, 0))],
        out_shape=[jax.ShapeDtypeStruct((depth, d, n_a), BF16),
                   jax.ShapeDtypeStruct((depth, d, n_b), BF16)],
        compiler_params=_params("parallel", "parallel"),
        name="split_w_in",
    )(w_in)


def _prep_q(w_qb):
    w = w_qb.astype(BF16).reshape(-1, Q_RANK, N_HEADS, QK_NOPE + QK_ROPE)
    nope = w[..., :QK_NOPE]
    q1 = w[..., QK_NOPE:QK_NOPE + HALF]
    q2 = w[..., QK_NOPE + HALF:]
    full = jnp.concatenate([nope, q1, q2, q2, q1], axis=3)
    return jnp.transpose(full, (0, 2, 3, 1))


def _prep_kv(w_kvb):
    w = w_kvb.astype(BF16).reshape(-1, KV_RANK, N_HEADS, QK_NOPE + V_DIM)
    wk = w[..., :QK_NOPE].reshape(-1, KV_RANK, N_HEADS * QK_NOPE)
    wv_t = jnp.transpose(w[..., QK_NOPE:].reshape(-1, KV_RANK, N_HEADS * V_DIM), (0, 2, 1))
    return wk, wv_t


def kernel(x, p, positions, g_mix, w_in, g_q, w_qb, g_kv, w_kvb, conv_w, conv_b, w_a, b_a, w_x, b_x,
           lru_lambda, w_out, g_mlp, w_up, w_down, g_ple, w_ple_gate, w_ple_proj, g_final):
    batch, seq, d = x.shape
    depth = w_in.shape[0]
    t = batch * seq
    assert d == D_MODEL and seq % TOK == 0

    ct_tab, st_tab, c_tab, s_tab = _rope_tables(positions)
    x2 = x.reshape(t, d)
    p3 = p.reshape(depth, t, PLE_DIM)
    row = lambda v: v.reshape(1, -1)

    w_in_a, w_in_b = _prep_in_proj(w_in)
    wq_t = _prep_q(w_qb)
    wk, wv_t = _prep_kv(w_kvb)
    w_ax = jnp.concatenate([w_a.astype(BF16), w_x.astype(BF16)], axis=-1)
    w_out, w_up, w_down, w_ple_gate, w_ple_proj = (
        w.astype(BF16) for w in (w_out, w_up, w_down, w_ple_gate, w_ple_proj))

    for l in range(depth):
        u, cq, ckv, kpe = _mla_down(x2, row(g_mix[l]), w_in_a, row(g_q[l]), row(g_kv[l]), c_tab, s_tab, l)
        q_t, k, v_t = _mla_up(cq, ckv, ct_tab, st_tab, wq_t, wk, wv_t, l)
        attn = _attention(q_t, k, kpe, v_t, batch, seq)

        zb = _in_proj(u, w_in_b, l)
        rec = _rglru(zb, conv_w[l], row(conv_b[l]), w_ax, row(b_a[l]), row(b_x[l]),
                     row(lru_lambda[l]), batch, seq, l)

        x2 = _out_proj(attn, rec, zb, x2, w_out, l)
        x2 = _mlp(x2, row(g_mlp[l]), w_up, w_down, l)
        x2 = _ple(x2, row(g_ple[l]), w_ple_gate, p3, w_ple_proj, row(g_final), l,
                  final_norm=(l == depth - 1))

    return x2.reshape(batch, seq, d)
```
